```python
import jax, jax.numpy as jnp
from jax import lax
import numpy as np

D_MODEL = 2048
BATCH = 8
SEQ = 2048
DEPTH = 1

HEAD_DIM = 128
N_Q_HEADS = 8
N_KV_HEADS = 2
Q_GROUP = N_Q_HEADS // N_KV_HEADS
ATTN_WIDTH = N_Q_HEADS * HEAD_DIM
KV_WIDTH = N_KV_HEADS * HEAD_DIM
WINDOW = 128
BLOCK = 128
ROPE_THETA = 10000.0
CONV_WIDTH = D_MODEL // 2
CONV_K = 3
N_BRANCH = 2
N_MEM = 256
MEM_HEADS = 4
MEM_HEAD_DIM = 128
MEM_WIDTH = MEM_HEADS * MEM_HEAD_DIM
D_FF = -(-8 * D_MODEL // (3 * 256)) * 256
RMS_EPS = 1e-6
NEG_INF = -1e30

IN_SIZES = (ATTN_WIDTH, KV_WIDTH, KV_WIDTH, CONV_WIDTH, CONV_WIDTH, CONV_WIDTH, N_BRANCH * D_MODEL)
IN_WIDTH = sum(IN_SIZES)
IN_SPLITS = tuple(int(i) for i in np.cumsum(IN_SIZES)[:-1])

kernel_name = "hybrid_gated_swa_shortconv_encoder"


def rms_norm(t, g):
    tf = t.astype(jnp.float32)
    y = tf * lax.rsqrt(jnp.mean(tf * tf, axis=-1, keepdims=True) + RMS_EPS)
    return (y * g.astype(jnp.float32)).astype(t.dtype)


def rope_tables(s):
    inv = 1.0 / (ROPE_THETA ** (jnp.arange(0, HEAD_DIM, 2, dtype=jnp.float32) / HEAD_DIM))
    ang = jnp.arange(s, dtype=jnp.float32)[:, None] * inv[None, :]
    return jnp.cos(ang), jnp.sin(ang)


def apply_rope(t, cos, sin):
    half = HEAD_DIM // 2
    t1, t2 = t[..., :half], t[..., half:]
    c = cos[None, :, None, :].astype(t.dtype)
    s = sin[None, :, None, :].astype(t.dtype)
    return jnp.concatenate([t1 * c - t2 * s, t2 * c + t1 * s], axis=-1)


def windowed_gqa_sink(q, k, v, sink):
    b, s = q.shape[0], q.shape[1]
    nb = s // BLOCK
    scale = HEAD_DIM ** -0.5
    qb = q.reshape(b, nb, BLOCK, N_KV_HEADS, Q_GROUP, HEAD_DIM)

    def band(t):
        tb = t.reshape(b, nb, BLOCK, N_KV_HEADS, HEAD_DIM)
        tp = jnp.pad(tb, ((0, 0), (1, 1), (0, 0), (0, 0), (0, 0)))
        return jnp.concatenate([tp[:, :-2], tp[:, 1:-1], tp[:, 2:]], axis=2)

    kb, vb = band(k), band(v)
    q_pos = jnp.arange(BLOCK)[:, None]
    k_off = jnp.arange(3 * BLOCK)[None, :] - BLOCK
    in_window = jnp.abs(k_off - q_pos) <= WINDOW
    k_abs = jnp.arange(nb)[:, None] * BLOCK + k_off
    in_range = (k_abs >= 0) & (k_abs < s)
    valid = in_window[None] & in_range[:, None, :]

    scores = jnp.einsum('bnqhgd,bnkhd->bnhgqk', qb, kb).astype(jnp.float32) * scale
    scores = jnp.where(valid[None, :, None, None], scores, NEG_INF)
    sink_col = jnp.broadcast_to(
        sink.astype(jnp.float32).reshape(1, 1, N_KV_HEADS, Q_GROUP, 1, 1),
        scores.shape[:-1] + (1,))
    probs = jax.nn.softmax(jnp.concatenate([scores, sink_col], axis=-1), axis=-1)[..., :-1]
    out = jnp.einsum('bnhgqk,bnkhd->bnqhgd', probs.astype(v.dtype), vb)
    return out.reshape(b, s, ATTN_WIDTH)


def short_conv_centred(u, w):
    up = jnp.pad(u, ((0, 0), (1, 1), (0, 0)))
    return up[:, :-2] * w[0] + up[:, 1:-1] * w[1] + up[:, 2:] * w[2]


def memory_cross_attention(h, mem_n, w_cq, w_ckv, w_co):
    b, s = h.shape[0], h.shape[1]
    m = mem_n.shape[1]
    q = (h @ w_cq).reshape(b, s, MEM_HEADS, MEM_HEAD_DIM)
    k, v = jnp.split(mem_n @ w_ckv, 2, axis=-1)
    k = k.reshape(b, m, MEM_HEADS, MEM_HEAD_DIM)
    v = v.reshape(b, m, MEM_HEADS, MEM_HEAD_DIM)
    scores = jnp.einsum('bshd,bmhd->bhsm', q, k).astype(jnp.float32) * (MEM_HEAD_DIM ** -0.5)
    probs = jax.nn.softmax(scores, axis=-1).astype(v.dtype)
    out = jnp.einsum('bhsm,bmhd->bshd', probs, v).reshape(b, s, MEM_WIDTH)
    return out @ w_co


def setup_inputs(seed: int = 0) -> dict:
    key = jax.random.key(seed)
    ks = jax.random.split(key, 24)
    f32 = jnp.float32

    def w(k, shape, fan_in):
        return jax.random.normal(k, shape, f32) * (fan_in ** -0.5)

    def gain(k, shape):
        return 1.0 + 0.02 * jax.random.normal(k, shape, f32)

    L = DEPTH
    return {
        "x": jax.random.normal(ks[0], (BATCH, SEQ, D_MODEL), f32),
        "mem": jax.random.normal(ks[1], (BATCH, N_MEM, D_MODEL), f32),
        "g_mix": gain(ks[2], (L, D_MODEL)),
        "w_in": w(ks[3], (L, D_MODEL, IN_WIDTH), D_MODEL),
        "sink": 0.5 * jax.random.normal(ks[4], (L, N_Q_HEADS), f32),
        "conv_w": w(ks[5], (L, CONV_K, CONV_WIDTH), CONV_K),
        "b_gate": 0.1 * jax.random.normal(ks[6], (L, N_BRANCH * D_MODEL), f32),
        "w_attn_out": w(ks[7], (L, ATTN_WIDTH, D_MODEL), ATTN_WIDTH),
        "w_conv_out": w(ks[8], (L, CONV_WIDTH, D_MODEL), CONV_WIDTH),
        "w_o": w(ks[9], (L, D_MODEL, D_MODEL), D_MODEL),
        "g_cross": gain(ks[10], (L, D_MODEL)),
        "g_mem": gain(ks[11], (L, D_MODEL)),
        "w_cq": w(ks[12], (L, D_MODEL, MEM_WIDTH), D_MODEL),
        "w_ckv": w(ks[13], (L, D_MODEL, 2 * MEM_WIDTH), D_MODEL),
        "w_co": w(ks[14], (L, MEM_WIDTH, D_MODEL), MEM_WIDTH),
        "g_ffn": gain(ks[15], (L, D_MODEL)),
        "w_gate": w(ks[16], (L, D_MODEL, D_FF), D_MODEL),
        "w_up": w(ks[17], (L, D_MODEL, D_FF), D_MODEL),
        "w_down": w(ks[18], (L, D_FF, D_MODEL), D_FF),
        "g_final": gain(ks[19], (D_MODEL,)),
    }


def reference(x, mem, g_mix, w_in, sink, conv_w, b_gate, w_attn_out, w_conv_out, w_o,
              g_cross, g_mem, w_cq, w_ckv, w_co, g_ffn, w_gate, w_up, w_down, g_final):
    b, s = x.shape[0], x.shape[1]
    cos, sin = rope_tables(s)
    for l in range(DEPTH):
        h = rms_norm(x, g_mix[l])
        z = h @ w_in[l]
        q, k, v, cu, cb, cc, gl = jnp.split(z, IN_SPLITS, axis=-1)

        q = apply_rope(q.reshape(b, s, N_Q_HEADS, HEAD_DIM), cos, sin)
        k = apply_rope(k.reshape(b, s, N_KV_HEADS, HEAD_DIM), cos, sin)
        v = v.reshape(b, s, N_KV_HEADS, HEAD_DIM)
        y_attn = windowed_gqa_sink(q, k, v, sink[l]) @ w_attn_out[l]

        y_conv = (cb * short_conv_centred(cc * cu, conv_w[l])) @ w_conv_out[l]

        g_a, g_c = jnp.split(jax.nn.sigmoid(gl + b_gate[l]), 2, axis=-1)
        x = x + (g_a * y_attn + g_c * y_conv) @ w_o[l]

        x = x + memory_cross_attention(rms_norm(x, g_cross[l]), rms_norm(mem, g_mem[l]),
                                       w_cq[l], w_ckv[l], w_co[l])

        hf = rms_norm(x, g_ffn[l])
        x = x + (jax.nn.silu(hf @ w_gate[l]) * (hf @ w_up[l])) @ w_down[l]
    return rms_norm(x, g_final)
```

```python
import functools

import jax
import jax.numpy as jnp
from jax import lax
from jax.experimental import pallas as pl
from jax.experimental.pallas import tpu as pltpu

HEAD_DIM = 128
N_Q_HEADS = 8
N_KV_HEADS = 2
Q_GROUP = N_Q_HEADS // N_KV_HEADS
WINDOW_BLOCK = 128
ROPE_THETA = 10000.0
MEM_HEADS = 4
MEM_HEAD_DIM = 128
RMS_EPS = 1e-6
NEG_INF = -1e30

V7X_VMEM_BYTES = 64 * 1024 * 1024
V7X_LANES = 128
BF16_SUBLANE_TILE = 16

F32 = jnp.float32
BF16 = jnp.bfloat16


def _vmem_limit(estimate_bytes):
    return int(min(estimate_bytes + 16 * 1024 * 1024, V7X_VMEM_BYTES - 8 * 1024 * 1024))


def _resident(block_shape, index_map):
    return pl.BlockSpec(block_shape, index_map, pipeline_mode=pl.Buffered(1))


def _rms_norm_f32(x, g):
    y = x * lax.rsqrt(jnp.mean(x * x, axis=-1, keepdims=True) + RMS_EPS)
    return y * g


def _norm_matmul_kernel(x_ref, g_ref, w_ref, o_ref, h_ref):
    @pl.when(pl.program_id(1) == 0)
    def _():
        h_ref[...] = _rms_norm_f32(x_ref[...], g_ref[...]).astype(BF16)

    o_ref[...] = jnp.dot(h_ref[...], w_ref[...], preferred_element_type=F32).astype(o_ref.dtype)


def _norm_matmul(x2d, g, w_bf16, *, tm, tn):
    m, d = x2d.shape
    n = w_bf16.shape[1]
    est = 2 * tm * d * 4 + tm * d * 2 + 2 * d * tn * 2 + 2 * tm * tn * 2
    return pl.pallas_call(
        _norm_matmul_kernel,
        out_shape=jax.ShapeDtypeStruct((m, n), BF16),
        grid=(m // tm, n // tn),
        in_specs=[
            pl.BlockSpec((tm, d), lambda i, j: (i, 0)),
            pl.BlockSpec((1, d), lambda i, j: (0, 0)),
            pl.BlockSpec((d, tn), lambda i, j: (0, j)),
        ],
        out_specs=pl.BlockSpec((tm, tn), lambda i, j: (i, j)),
        scratch_shapes=[pltpu.VMEM((tm, d), BF16)],
        compiler_params=pltpu.CompilerParams(
            dimension_semantics=("arbitrary", "arbitrary"),
            vmem_limit_bytes=_vmem_limit(est)),
        name="norm_matmul",
    )(x2d, g.reshape(1, d), w_bf16)


def _rope(t, cos_full, sin_signed):
    return t * cos_full + pltpu.roll(t, HEAD_DIM // 2, axis=1) * sin_signed


def _local_mixer_kernel(sink_ref, q_ref, kvc_ref, kvp_ref, kvn_ref,
                        cu_ref, cb_ref, cc_ref, cup_ref, ccp_ref, cun_ref, ccn_ref,
                        cw_ref, cos_ref, sin_ref,
                        attn_ref, conv_ref,
                        qs_ref, kb_ref, vb_ref, *, tq, seq):
    i = pl.program_id(1)
    n_tiles = seq // tq
    nblk = tq // WINDOW_BLOCK
    blk = WINDOW_BLOCK
    hd = HEAD_DIM
    scale = HEAD_DIM ** -0.5

    row0 = pl.multiple_of(i * tq, tq)
    prev0 = pl.multiple_of(jnp.maximum(i * tq - blk, 0), blk)
    next0 = pl.multiple_of(jnp.minimum((i + 1) * tq, seq - blk), blk)
    cos_c, sin_c = cos_ref[pl.ds(row0, tq), :], sin_ref[pl.ds(row0, tq), :]
    cos_p, sin_p = cos_ref[pl.ds(prev0, blk), :], sin_ref[pl.ds(prev0, blk), :]
    cos_n, sin_n = cos_ref[pl.ds(next0, blk), :], sin_ref[pl.ds(next0, blk), :]

    for h in range(N_KV_HEADS):
        ks = slice(h * hd, (h + 1) * hd)
        vs = slice((N_KV_HEADS + h) * hd, (N_KV_HEADS + h + 1) * hd)
        kb_ref[h, 0:blk, :] = _rope(kvp_ref[0, :, ks].astype(F32), cos_p, sin_p).astype(BF16)
        kb_ref[h, blk:blk + tq, :] = _rope(kvc_ref[0, :, ks].astype(F32), cos_c, sin_c).astype(BF16)
        kb_ref[h, blk + tq:, :] = _rope(kvn_ref[0, :, ks].astype(F32), cos_n, sin_n).astype(BF16)
        vb_ref[h, 0:blk, :] = kvp_ref[0, :, vs]
        vb_ref[h, blk:blk + tq, :] = kvc_ref[0, :, vs]
        vb_ref[h, blk + tq:, :] = kvn_ref[0, :, vs]

    for qh in range(N_Q_HEADS):
        h, g = divmod(qh, Q_GROUP)
        t = _rope(q_ref[0, :, qh * hd:(qh + 1) * hd].astype(F32), cos_c, sin_c) * scale
        t = t.astype(BF16)
        for j in range(nblk):
            qs_ref[h, j, g * blk:(g + 1) * blk, :] = t[j * blk:(j + 1) * blk, :]

    rows = Q_GROUP * blk
    qp = lax.broadcasted_iota(jnp.int32, (rows, 3 * blk), 0) % blk
    col = lax.broadcasted_iota(jnp.int32, (rows, 3 * blk), 1)
    part = col // blk
    kp = col % blk
    in_window = ((part == 0) & (kp >= qp)) | (part == 1) | ((part == 2) & (kp <= qp))
    has_prev = i > 0
    has_next = i < n_tiles - 1

    for h in range(N_KV_HEADS):
        sink_col = jnp.concatenate(
            [jnp.full((blk, 1), sink_ref[h * Q_GROUP + g], F32) for g in range(Q_GROUP)], axis=0)
        for j in range(nblk):
            valid = in_window
            if j == 0:
                valid = valid & ((part != 0) | has_prev)
            if j == nblk - 1:
                valid = valid & ((part != 2) | has_next)
            k_band = kb_ref[h, j * blk:(j + 3) * blk, :]
            v_band = vb_ref[h, j * blk:(j + 3) * blk, :]
            s = lax.dot_general(qs_ref[h, j], k_band, (((1,), (1,)), ((), ())),
                                preferred_element_type=F32)
            s = jnp.where(valid, s, NEG_INF)
            m = jnp.maximum(jnp.max(s, axis=-1, keepdims=True), sink_col)
            e = jnp.exp(s - m)
            denom = jnp.sum(e, axis=-1, keepdims=True) + jnp.exp(sink_col - m)
            o = jnp.dot(e.astype(BF16), v_band, preferred_element_type=F32) / denom
            for g in range(Q_GROUP):
                qh = h * Q_GROUP + g
                attn_ref[0, j * blk:(j + 1) * blk, qh * hd:(qh + 1) * hd] = (
                    o[g * blk:(g + 1) * blk, :].astype(BF16))

    u = cc_ref[0].astype(F32) * cu_ref[0].astype(F32)
    last = BF16_SUBLANE_TILE - 1
    u_prev = ccp_ref[0, last:last + 1, :].astype(F32) * cup_ref[0, last:last + 1, :].astype(F32)
    u_next = ccn_ref[0, 0:1, :].astype(F32) * cun_ref[0, 0:1, :].astype(F32)
    u_prev = jnp.where(has_prev, u_prev, 0.0)
    u_next = jnp.where(has_next, u_next, 0.0)
    t_idx = lax.broadcasted_iota(jnp.int32, u.shape, 0)
    u_dn = jnp.where(t_idx == 0, u_prev, pltpu.roll(u, 1, axis=0))
    u_up = jnp.where(t_idx == tq - 1, u_next, pltpu.roll(u, tq - 1, axis=0))
    conv = u_dn * cw_ref[0:1, :] + u * cw_ref[1:2, :] + u_up * cw_ref[2:3, :]
    conv_ref[0] = (cb_ref[0].astype(F32) * conv).astype(BF16)


def _local_mixer(z3, sink, conv_w, cos_full, sin_signed, *, tq):
    b, s, _ = z3.shape
    aw = N_Q_HEADS * HEAD_DIM
    kvw = 2 * N_KV_HEADS * HEAD_DIM
    kv_col = (4 * aw + 2 * 2 * aw) // kvw
    nblk = tq // WINDOW_BLOCK
    n_win = s // WINDOW_BLOCK
    sub = BF16_SUBLANE_TILE
    n_sub = s // sub

    def tile(col):
        return pl.BlockSpec((1, tq, aw), lambda bi, i, col=col: (bi, i, col))

    def halo_prev(col):
        return pl.BlockSpec((1, sub, aw),
                            lambda bi, i, col=col: (bi, jnp.maximum(i * (tq // sub) - 1, 0), col))

    def halo_next(col):
        return pl.BlockSpec((1, sub, aw),
                            lambda bi, i, col=col: (bi, jnp.minimum((i + 1) * (tq // sub), n_sub - 1), col))

    in_specs = [
        pl.BlockSpec(memory_space=pltpu.SMEM),
        tile(0),
        pl.BlockSpec((1, tq, kvw), lambda bi, i: (bi, i, kv_col)),
        pl.BlockSpec((1, WINDOW_BLOCK, kvw),
                     lambda bi, i: (bi, jnp.maximum(i * nblk - 1, 0), kv_col)),
        pl.BlockSpec((1, WINDOW_BLOCK, kvw),
                     lambda bi, i: (bi, jnp.minimum((i + 1) * nblk, n_win - 1), kv_col)),
        tile(1), tile(2), tile(3),
        halo_prev(1), halo_prev(3), halo_next(1), halo_next(3),
        _resident((3, aw), lambda bi, i: (0, 0)),
        _resident((s, HEAD_DIM), lambda bi, i: (0, 0)),
        _resident((s, HEAD_DIM), lambda bi, i: (0, 0)),
    ]
    out_specs = [pl.BlockSpec((1, tq, aw), lambda bi, i: (bi, i, 0))] * 2
    est = (2 * (4 * tq * aw + tq * kvw + 2 * WINDOW_BLOCK * kvw + 4 * sub * aw) * 2
           + 2 * 2 * tq * aw * 2 + 2 * s * HEAD_DIM * 4
           + (N_KV_HEADS * nblk * Q_GROUP * WINDOW_BLOCK * HEAD_DIM
              + 2 * N_KV_HEADS * (tq + 2 * WINDOW_BLOCK) * HEAD_DIM) * 2
           + 6 * tq * aw * 4)
    return pl.pallas_call(
        functools.partial(_local_mixer_kernel, tq=tq, seq=s),
        out_shape=[jax.ShapeDtypeStruct((b, s, aw), BF16)] * 2,
        grid=(b, s // tq),
        in_specs=in_specs,
        out_specs=out_specs,
        scratch_shapes=[
            pltpu.VMEM((N_KV_HEADS, nblk, Q_GROUP * WINDOW_BLOCK, HEAD_DIM), BF16),
            pltpu.VMEM((N_KV_HEADS, tq + 2 * WINDOW_BLOCK, HEAD_DIM), BF16),
            pltpu.VMEM((N_KV_HEADS, tq + 2 * WINDOW_BLOCK, HEAD_DIM), BF16),
        ],
        compiler_params=pltpu.CompilerParams(
            dimension_semantics=("arbitrary", "arbitrary"),
            vmem_limit_bytes=_vmem_limit(est)),
        name="local_mixer",
    )(sink, z3, z3, z3, z3, z3, z3, z3, z3, z3, z3, z3, conv_w, cos_full, sin_signed)


def _merge_kernel(attn_ref, conv_ref, gla_ref, glc_ref, bg_ref, x_ref, wa_ref, wc_ref, wo_ref,
                  o_ref, m_ref, *, nc):
    d = x_ref.shape[1]
    for c0 in range(0, d, nc):
        cs = slice(c0, c0 + nc)
        ya = jnp.dot(attn_ref[...], wa_ref[:, cs], preferred_element_type=F32)
        yc = jnp.dot(conv_ref[...], wc_ref[:, cs], preferred_element_type=F32)
        ga = jax.nn.sigmoid(gla_ref[:, cs].astype(F32) + bg_ref[:, c0:c0 + nc])
        gc = jax.nn.sigmoid(glc_ref[:, cs].astype(F32) + bg_ref[:, d + c0:d + c0 + nc])
        m_ref[:, cs] = (ga * ya + gc * yc).astype(BF16)
    for c0 in range(0, d, nc):
        cs = slice(c0, c0 + nc)
        o_ref[:, cs] = x_ref[:, cs] + jnp.dot(m_ref[...], wo_ref[:, cs], preferred_element_type=F32)


def _merge(attn2d, conv2d, z2d, b_gate, x2d, wa, wc, wo, *, tm, nc):
    m, d = x2d.shape
    aw = attn2d.shape[1]
    ga_col = 4 * aw // d
    est = (2 * (2 * tm * aw * 2 + 2 * tm * d * 2 + 2 * tm * d * 4) + tm * d * 2
           + (2 * aw * d + d * d) * 2 + 3 * tm * nc * 4)
    return pl.pallas_call(
        functools.partial(_merge_kernel, nc=nc),
        out_shape=jax.ShapeDtypeStruct((m, d), F32),
        grid=(m // tm,),
        in_specs=[
            pl.BlockSpec((tm, aw), lambda i: (i, 0)),
            pl.BlockSpec((tm, aw), lambda i: (i, 0)),
            pl.BlockSpec((tm, d), lambda i: (i, ga_col)),
            pl.BlockSpec((tm, d), lambda i: (i, ga_col + 1)),
            _resident((1, 2 * d), lambda i: (0, 0)),
            pl.BlockSpec((tm, d), lambda i: (i, 0)),
            _resident((aw, d), lambda i: (0, 0)),
            _resident((aw, d), lambda i: (0, 0)),
            _resident((d, d), lambda i: (0, 0)),
        ],
        out_specs=pl.BlockSpec((tm, d), lambda i: (i, 0)),
        scratch_shapes=[pltpu.VMEM((tm, d), BF16)],
        compiler_params=pltpu.CompilerParams(
            dimension_semantics=("arbitrary",),
            vmem_limit_bytes=_vmem_limit(est)),
        name="merge_proj",
    )(attn2d, conv2d, z2d, z2d, b_gate.reshape(1, 2 * d), x2d, wa, wc, wo)


def _cross_kernel(x_ref, g_ref, wq_ref, kv_ref, wo_ref, o_ref, a_ref, *, nc):
    hd = MEM_HEAD_DIM
    mw = MEM_HEADS * hd
    scale = MEM_HEAD_DIM ** -0.5
    x = x_ref[...]
    h = _rms_norm_f32(x, g_ref[...]).astype(BF16)
    q = (jnp.dot(h, wq_ref[...], preferred_element_type=F32) * scale).astype(BF16)
    for hh in range(MEM_HEADS):
        k = kv_ref[0, :, hh * hd:(hh + 1) * hd]
        v = kv_ref[0, :, mw + hh * hd:mw + (hh + 1) * hd]
        s = lax.dot_general(q[:, hh * hd:(hh + 1) * hd], k, (((1,), (1,)), ((), ())),
                            preferred_element_type=F32)
        e = jnp.exp(s - jnp.max(s, axis=-1, keepdims=True))
        denom = jnp.sum(e, axis=-1, keepdims=True)
        o = jnp.dot(e.astype(BF16), v, preferred_element_type=F32) / denom
        a_ref[:, hh * hd:(hh + 1) * hd] = o.astype(BF16)
    d = x_ref.shape[1]
    for c0 in range(0, d, nc):
        cs = slice(c0, c0 + nc)
        o_ref[:, cs] = x_ref[:, cs] + jnp.dot(a_ref[...], wo_ref[:, cs], preferred_element_type=F32)


def _cross(x2d, g, wq, kv3, wo, *, tm, seq, nc):
    m, d = x2d.shape
    n_mem, kvw = kv3.shape[1], kv3.shape[2]
    mw = wq.shape[1]
    tiles_per_batch = seq // tm
    est = (2 * 2 * tm * d * 4 + 2 * n_mem * kvw * 2 + 2 * d * mw * 2 + tm * mw * 2
           + tm * d * 4 + tm * d * 2 + 2 * tm * n_mem * 4 + tm * nc * 4)
    return pl.pallas_call(
        functools.partial(_cross_kernel, nc=nc),
        out_shape=jax.ShapeDtypeStruct((m, d), F32),
        grid=(m // tm,),
        in_specs=[
            pl.BlockSpec((tm, d), lambda i: (i, 0)),
            _resident((1, d), lambda i: (0, 0)),
            _resident((d, mw), lambda i: (0, 0)),
            pl.BlockSpec((1, n_mem, kvw), lambda i: (i // tiles_per_batch, 0, 0)),
            _resident((mw, d), lambda i: (0, 0)),
        ],
        out_specs=pl.BlockSpec((tm, d), lambda i: (i, 0)),
        scratch_shapes=[pltpu.VMEM((tm, mw), BF16)],
        compiler_params=pltpu.CompilerParams(
            dimension_semantics=("arbitrary",),
            vmem_limit_bytes=_vmem_limit(est)),
        name="cross_attn",
    )(x2d, g.reshape(1, d), wq, kv3, wo)


def _ffn_kernel(x_ref, g_ref, wg_ref, wu_ref, wd_ref, gf_ref, o_ref, h_ref, acc_ref, *, final_norm):
    f = pl.program_id(1)

    @pl.when(f == 0)
    def _():
        h_ref[...] = _rms_norm_f32(x_ref[...], g_ref[...]).astype(BF16)

    gate = jnp.dot(h_ref[...], wg_ref[...], preferred_element_type=F32)
    up = jnp.dot(h_ref[...], wu_ref[...], preferred_element_type=F32)
    a = (jax.nn.silu(gate) * up).astype(BF16)
    contrib = jnp.dot(a, wd_ref[...], preferred_element_type=F32)

    @pl.when(f == 0)
    def _():
        acc_ref[...] = contrib

    @pl.when(f > 0)
    def _():
        acc_ref[...] += contrib

    @pl.when(f == pl.num_programs(1) - 1)
    def _():
        y = x_ref[...] + acc_ref[...]
        if final_norm:
            y = _rms_norm_f32(y, gf_ref[...])
        o_ref[...] = y


def _ffn(x2d, g, wg, wu, wd, g_final, *, tm, tf, final_norm):
    m, d = x2d.shape
    dff = wg.shape[1]
    est = (2 * 2 * tm * d * 4 + tm * d * 4 + tm * d * 2 + 2 * 3 * d * tf * 2
           + 3 * tm * tf * 4 + tm * d * 4)
    return pl.pallas_call(
        functools.partial(_ffn_kernel, final_norm=final_norm),
        out_shape=jax.ShapeDtypeStruct((m, d), F32),
        grid=(m // tm, dff // tf),
        in_specs=[
            pl.BlockSpec((tm, d), lambda i, f: (i, 0)),
            _resident((1, d), lambda i, f: (0, 0)),
            pl.BlockSpec((d, tf), lambda i, f: (0, f)),
            pl.BlockSpec((d, tf), lambda i, f: (0, f)),
            pl.BlockSpec((tf, d), lambda i, f: (f, 0)),
            _resident((1, d), lambda i, f: (0, 0)),
        ],
        out_specs=pl.BlockSpec((tm, d), lambda i, f: (i, 0)),
        scratch_shapes=[pltpu.VMEM((tm, d), BF16), pltpu.VMEM((tm, d), F32)],
        compiler_params=pltpu.CompilerParams(
            dimension_semantics=("arbitrary", "arbitrary"),
            vmem_limit_bytes=_vmem_limit(est)),
        name="swiglu_ffn",
    )(x2d, g.reshape(1, d), wg, wu, wd, g_final.reshape(1, d))


def _rope_tables(s):
    inv = 1.0 / (ROPE_THETA ** (jnp.arange(0, HEAD_DIM, 2, dtype=F32) / HEAD_DIM))
    ang = jnp.arange(s, dtype=F32)[:, None] * inv[None, :]
    cos, sin = jnp.cos(ang), jnp.sin(ang)
    return jnp.concatenate([cos, cos], axis=-1), jnp.concatenate([-sin, sin], axis=-1)


def _permute_in_proj(w_in):
    aw = N_Q_HEADS * HEAD_DIM
    kw = N_KV_HEADS * HEAD_DIM
    q_end = aw
    kv_end = aw + 2 * kw
    return jnp.concatenate([w_in[:, :q_end], w_in[:, kv_end:], w_in[:, q_end:kv_end]], axis=1)


def kernel(x, mem, g_mix, w_in, sink, conv_w, b_gate, w_attn_out, w_conv_out, w_o,
           g_cross, g_mem, w_cq, w_ckv, w_co, g_ffn, w_gate, w_up, w_down, g_final):
    b, s, d = x.shape
    n_mem = mem.shape[1]
    depth = g_mix.shape[0]
    cos_full, sin_signed = _rope_tables(s)
    x2d = x.reshape(b * s, d)
    mem2d = mem.reshape(b * n_mem, d)
    tm = min(512, s)
    for l in range(depth):
        w_in_l = _permute_in_proj(w_in[l]).astype(BF16)
        z = _norm_matmul(x2d, g_mix[l], w_in_l, tm=min(1024, s), tn=512)
        attn, conv = _local_mixer(z.reshape(b, s, -1), sink[l], conv_w[l], cos_full, sin_signed,
                                  tq=min(512, s))
        x2d = _merge(attn.reshape(b * s, -1), conv.reshape(b * s, -1), z, b_gate[l], x2d,
                     w_attn_out[l].astype(BF16), w_conv_out[l].astype(BF16), w_o[l].astype(BF16),
                     tm=tm, nc=512)
        mem_kv = _norm_matmul(mem2d, g_mem[l], w_ckv[l].astype(BF16),
                              tm=min(1024, b * n_mem), tn=w_ckv.shape[2])
        x2d = _cross(x2d, g_cross[l], w_cq[l].astype(BF16), mem_kv.reshape(b, n_mem, -1),
                     w_co[l].astype(BF16), tm=tm, seq=s, nc=512)
        x2d = _ffn(x2d, g_ffn[l], w_gate[l].astype(BF16), w_up[l].astype(BF16),
                   w_down[l].astype(BF16), g_final, tm=tm, tf=512,
                   final_norm=(l == depth - 1))
    return x2d.reshape(b, s, d)
```

```python
import functools
import math

import jax
import jax.numpy as jnp
from jax import lax
from jax.experimental import pallas as pl
from jax.experimental.pallas import tpu as pltpu

HEAD_DIM = 128
N_Q_HEADS = 8
N_KV_HEADS = 2
Q_GROUP = N_Q_HEADS // N_KV_HEADS
WINDOW_BLOCK = 128
ROPE_THETA = 10000.0
MEM_HEADS = 4
MEM_HEAD_DIM = 128
RMS_EPS = 1e-6
NEG_INF = -1e30
LOG2_E = math.log2(math.e)

V7X_VMEM_BYTES = 64 * 1024 * 1024
V7X_LANES = 128
BF16_SUBLANE_TILE = 16

F32 = jnp.float32
BF16 = jnp.bfloat16


def _vmem_limit(estimate_bytes):
    return int(min(estimate_bytes + 16 * 1024 * 1024, V7X_VMEM_BYTES - 8 * 1024 * 1024))


def _resident(block_shape, index_map):
    return pl.BlockSpec(block_shape, index_map, pipeline_mode=pl.Buffered(1))


def _rms_norm_f32(x, g):
    y = x * lax.rsqrt(jnp.mean(x * x, axis=-1, keepdims=True) + RMS_EPS)
    return y * g


def _norm_matmul_kernel(x_ref, g_ref, w_ref, o_ref, h_ref):
    @pl.when(pl.program_id(1) == 0)
    def _():
        h_ref[...] = _rms_norm_f32(x_ref[...], g_ref[...]).astype(BF16)

    o_ref[...] = jnp.dot(h_ref[...], w_ref[...], preferred_element_type=F32).astype(o_ref.dtype)


def _norm_matmul(x2d, g, w_bf16, *, tm, tn):
    m, d = x2d.shape
    n = w_bf16.shape[1]
    est = 2 * tm * d * 4 + tm * d * 2 + 2 * d * tn * 2 + 2 * tm * tn * 2
    return pl.pallas_call(
        _norm_matmul_kernel,
        out_shape=jax.ShapeDtypeStruct((m, n), BF16),
        grid=(m // tm, n // tn),
        in_specs=[
            pl.BlockSpec((tm, d), lambda i, j: (i, 0)),
            pl.BlockSpec((1, d), lambda i, j: (0, 0)),
            pl.BlockSpec((d, tn), lambda i, j: (0, j)),
        ],
        out_specs=pl.BlockSpec((tm, tn), lambda i, j: (i, j)),
        scratch_shapes=[pltpu.VMEM((tm, d), BF16)],
        compiler_params=pltpu.CompilerParams(
            dimension_semantics=("arbitrary", "arbitrary"),
            vmem_limit_bytes=_vmem_limit(est)),
        name="norm_matmul",
    )(x2d, g.reshape(1, d), w_bf16)


def _rope(t, cos_full, sin_signed):
    return t * cos_full + pltpu.roll(t, HEAD_DIM // 2, axis=1) * sin_signed


def _local_mixer_kernel(sink_ref, q_ref, kvc_ref, kvp_ref, kvn_ref,
                        cu_ref, cb_ref, cc_ref, cup_ref, ccp_ref, cun_ref, ccn_ref,
                        cw_ref, cos_ref, sin_ref,
                        attn_ref, conv_ref,
                        qs_ref, kb_ref, vb_ref, *, tq, seq):
    i = pl.program_id(1)
    n_tiles = seq // tq
    nblk = tq // WINDOW_BLOCK
    blk = WINDOW_BLOCK
    hd = HEAD_DIM
    scale = HEAD_DIM ** -0.5

    row0 = pl.multiple_of(i * tq, tq)
    prev0 = pl.multiple_of(jnp.maximum(i * tq - blk, 0), blk)
    next0 = pl.multiple_of(jnp.minimum((i + 1) * tq, seq - blk), blk)
    cos_c, sin_c = cos_ref[pl.ds(row0, tq), :], sin_ref[pl.ds(row0, tq), :]
    cos_p, sin_p = cos_ref[pl.ds(prev0, blk), :], sin_ref[pl.ds(prev0, blk), :]
    cos_n, sin_n = cos_ref[pl.ds(next0, blk), :], sin_ref[pl.ds(next0, blk), :]

    for h in range(N_KV_HEADS):
        ks = slice(h * hd, (h + 1) * hd)
        vs = slice((N_KV_HEADS + h) * hd, (N_KV_HEADS + h + 1) * hd)
        kb_ref[h, 0:blk, :] = _rope(kvp_ref[0, :, ks].astype(F32), cos_p, sin_p).astype(BF16)
        kb_ref[h, blk:blk + tq, :] = _rope(kvc_ref[0, :, ks].astype(F32), cos_c, sin_c).astype(BF16)
        kb_ref[h, blk + tq:, :] = _rope(kvn_ref[0, :, ks].astype(F32), cos_n, sin_n).astype(BF16)
        vb_ref[h, 0:blk, 0:hd] = kvp_ref[0, :, vs]
        vb_ref[h, blk:blk + tq, 0:hd] = kvc_ref[0, :, vs]
        vb_ref[h, blk + tq:, 0:hd] = kvn_ref[0, :, vs]
        vb_ref[h, :, hd:] = jnp.ones((tq + 2 * blk, hd), BF16)

    for qh in range(N_Q_HEADS):
        h, g = divmod(qh, Q_GROUP)
        t = _rope(q_ref[0, :, qh * hd:(qh + 1) * hd].astype(F32), cos_c, sin_c) * (scale * LOG2_E)
        t = t.astype(BF16)
        for j in range(nblk):
            qs_ref[h, j, g * blk:(g + 1) * blk, :] = t[j * blk:(j + 1) * blk, :]

    rows = Q_GROUP * blk
    qp = lax.broadcasted_iota(jnp.int32, (rows, blk), 0) % blk
    kp = lax.broadcasted_iota(jnp.int32, (rows, blk), 1)
    has_prev = i > 0
    has_next = i < n_tiles - 1
    bias_prev = jnp.where(kp >= qp, 0.0, NEG_INF)
    bias_next = jnp.where(kp <= qp, 0.0, NEG_INF)
    bias_prev_edge = jnp.where(has_prev, bias_prev, NEG_INF)
    bias_next_edge = jnp.where(has_next, bias_next, NEG_INF)

    for h in range(N_KV_HEADS):
        sink_b = jnp.concatenate(
            [jnp.full((blk, hd), sink_ref[h * Q_GROUP + g] * LOG2_E, F32) for g in range(Q_GROUP)],
            axis=0)
        for j in range(nblk):
            s = lax.dot_general(qs_ref[h, j], kb_ref[h, j * blk:(j + 3) * blk, :],
                                (((1,), (1,)), ((), ())), preferred_element_type=F32)
            s_prev = s[:, 0:blk] + (bias_prev_edge if j == 0 else bias_prev)
            s_cur = s[:, blk:2 * blk]
            s_next = s[:, 2 * blk:] + (bias_next_edge if j == nblk - 1 else bias_next)
            m = jnp.max(jnp.maximum(jnp.maximum(s_prev, s_cur), s_next), axis=-1, keepdims=True)
            m = jnp.maximum(m, sink_b)
            e = jnp.concatenate(
                [jnp.exp2(s_prev - m), jnp.exp2(s_cur - m), jnp.exp2(s_next - m)], axis=1)
            o_aug = jnp.dot(e.astype(BF16), vb_ref[h, j * blk:(j + 3) * blk, :],
                            preferred_element_type=F32)
            o = o_aug[:, 0:hd] / (o_aug[:, hd:] + jnp.exp2(sink_b - m))
            for g in range(Q_GROUP):
                qh = h * Q_GROUP + g
                attn_ref[0, j * blk:(j + 1) * blk, qh * hd:(qh + 1) * hd] = (
                    o[g * blk:(g + 1) * blk, :].astype(BF16))

    u = cc_ref[0].astype(F32) * cu_ref[0].astype(F32)
    last = BF16_SUBLANE_TILE - 1
    u_prev = ccp_ref[0, last:last + 1, :].astype(F32) * cup_ref[0, last:last + 1, :].astype(F32)
    u_next = ccn_ref[0, 0:1, :].astype(F32) * cun_ref[0, 0:1, :].astype(F32)
    u_prev = jnp.where(has_prev, u_prev, 0.0)
    u_next = jnp.where(has_next, u_next, 0.0)
    t_idx = lax.broadcasted_iota(jnp.int32, u.shape, 0)
    u_dn = jnp.where(t_idx == 0, u_prev, pltpu.roll(u, 1, axis=0))
    u_up = jnp.where(t_idx == tq - 1, u_next, pltpu.roll(u, tq - 1, axis=0))
    conv = u_dn * cw_ref[0:1, :] + u * cw_ref[1:2, :] + u_up * cw_ref[2:3, :]
    conv_ref[0] = (cb_ref[0].astype(F32) * conv).astype(BF16)


def _local_mixer(z3, sink, conv_w, cos_full, sin_signed, *, tq):
    b, s, _ = z3.shape
    aw = N_Q_HEADS * HEAD_DIM
    kvw = 2 * N_KV_HEADS * HEAD_DIM
    kv_col = (4 * aw + 2 * 2 * aw) // kvw
    nblk = tq // WINDOW_BLOCK
    n_win = s // WINDOW_BLOCK
    sub = BF16_SUBLANE_TILE
    n_sub = s // sub

    def tile(col):
        return pl.BlockSpec((1, tq, aw), lambda bi, i, col=col: (bi, i, col))

    def halo_prev(col):
        return pl.BlockSpec((1, sub, aw),
                            lambda bi, i, col=col: (bi, jnp.maximum(i * (tq // sub) - 1, 0), col))

    def halo_next(col):
        return pl.BlockSpec((1, sub, aw),
                            lambda bi, i, col=col: (bi, jnp.minimum((i + 1) * (tq // sub), n_sub - 1), col))

    in_specs = [
        pl.BlockSpec(memory_space=pltpu.SMEM),
        tile(0),
        pl.BlockSpec((1, tq, kvw), lambda bi, i: (bi, i, kv_col)),
        pl.BlockSpec((1, WINDOW_BLOCK, kvw),
                     lambda bi, i: (bi, jnp.maximum(i * nblk - 1, 0), kv_col)),
        pl.BlockSpec((1, WINDOW_BLOCK, kvw),
                     lambda bi, i: (bi, jnp.minimum((i + 1) * nblk, n_win - 1), kv_col)),
        tile(1), tile(2), tile(3),
        halo_prev(1), halo_prev(3), halo_next(1), halo_next(3),
        _resident((3, aw), lambda bi, i: (0, 0)),
        _resident((s, HEAD_DIM), lambda bi, i: (0, 0)),
        _resident((s, HEAD_DIM), lambda bi, i: (0, 0)),
    ]
    out_specs = [pl.BlockSpec((1, tq, aw), lambda bi, i: (bi, i, 0))] * 2
    est = (2 * (4 * tq * aw + tq * kvw + 2 * WINDOW_BLOCK * kvw + 4 * sub * aw) * 2
           + 2 * 2 * tq * aw * 2 + 2 * s * HEAD_DIM * 4
           + (N_KV_HEADS * nblk * Q_GROUP * WINDOW_BLOCK * HEAD_DIM
              + 2 * N_KV_HEADS * (tq + 2 * WINDOW_BLOCK) * HEAD_DIM) * 2
           + 6 * tq * aw * 4)
    return pl.pallas_call(
        functools.partial(_local_mixer_kernel, tq=tq, seq=s),
        out_shape=[jax.ShapeDtypeStruct((b, s, aw), BF16)] * 2,
        grid=(b, s // tq),
        in_specs=in_specs,
        out_specs=out_specs,
        scratch_shapes=[
            pltpu.VMEM((N_KV_HEADS, nblk, Q_GROUP * WINDOW_BLOCK, HEAD_DIM), BF16),
            pltpu.VMEM((N_KV_HEADS, tq + 2 * WINDOW_BLOCK, HEAD_DIM), BF16),
            pltpu.VMEM((N_KV_HEADS, tq + 2 * WINDOW_BLOCK, 2 * HEAD_DIM), BF16),
        ],
        compiler_params=pltpu.CompilerParams(
            dimension_semantics=("arbitrary", "arbitrary"),
            vmem_limit_bytes=_vmem_limit(est)),
        name="local_mixer",
    )(sink, z3, z3, z3, z3, z3, z3, z3, z3, z3, z3, z3, conv_w, cos_full, sin_signed)


def _merge_kernel(attn_ref, conv_ref, gla_ref, glc_ref, bg_ref, x_ref, wa_ref, wc_ref, wo_ref,
                  o_ref, m_ref, *, nc):
    d = x_ref.shape[1]
    for c0 in range(0, d, nc):
        cs = slice(c0, c0 + nc)
        ya = jnp.dot(attn_ref[...], wa_ref[:, cs], preferred_element_type=F32)
        yc = jnp.dot(conv_ref[...], wc_ref[:, cs], preferred_element_type=F32)
        ga = jax.nn.sigmoid(gla_ref[:, cs].astype(F32) + bg_ref[:, c0:c0 + nc])
        gc = jax.nn.sigmoid(glc_ref[:, cs].astype(F32) + bg_ref[:, d + c0:d + c0 + nc])
        m_ref[:, cs] = (ga * ya + gc * yc).astype(BF16)
    for c0 in range(0, d, nc):
        cs = slice(c0, c0 + nc)
        o_ref[:, cs] = x_ref[:, cs] + jnp.dot(m_ref[...], wo_ref[:, cs], preferred_element_type=F32)


def _merge(attn2d, conv2d, z2d, b_gate, x2d, wa, wc, wo, *, tm, nc):
    m, d = x2d.shape
    aw = attn2d.shape[1]
    ga_col = 4 * aw // d
    est = (2 * (2 * tm * aw * 2 + 2 * tm * d * 2 + 2 * tm * d * 4) + tm * d * 2
           + (2 * aw * d + d * d) * 2 + 3 * tm * nc * 4)
    return pl.pallas_call(
        functools.partial(_merge_kernel, nc=nc),
        out_shape=jax.ShapeDtypeStruct((m, d), F32),
        grid=(m // tm,),
        in_specs=[
            pl.BlockSpec((tm, aw), lambda i: (i, 0)),
            pl.BlockSpec((tm, aw), lambda i: (i, 0)),
            pl.BlockSpec((tm, d), lambda i: (i, ga_col)),
            pl.BlockSpec((tm, d), lambda i: (i, ga_col + 1)),
            _resident((1, 2 * d), lambda i: (0, 0)),
            pl.BlockSpec((tm, d), lambda i: (i, 0)),
            _resident((aw, d), lambda i: (0, 0)),
            _resident((aw, d), lambda i: (0, 0)),
            _resident((d, d), lambda i: (0, 0)),
        ],
        out_specs=pl.BlockSpec((tm, d), lambda i: (i, 0)),
        scratch_shapes=[pltpu.VMEM((tm, d), BF16)],
        compiler_params=pltpu.CompilerParams(
            dimension_semantics=("arbitrary",),
            vmem_limit_bytes=_vmem_limit(est)),
        name="merge_proj",
    )(attn2d, conv2d, z2d, z2d, b_gate.reshape(1, 2 * d), x2d, wa, wc, wo)


def _cross_kernel(x_ref, g_ref, wq_ref, kv_ref, wo_ref, o_ref, a_ref, *, nc):
    hd = MEM_HEAD_DIM
    mw = MEM_HEADS * hd
    scale = MEM_HEAD_DIM ** -0.5
    x = x_ref[...]
    h = _rms_norm_f32(x, g_ref[...]).astype(BF16)
    q = (jnp.dot(h, wq_ref[...], preferred_element_type=F32) * scale).astype(BF16)
    for hh in range(MEM_HEADS):
        k = kv_ref[0, :, hh * hd:(hh + 1) * hd]
        v = kv_ref[0, :, mw + hh * hd:mw + (hh + 1) * hd]
        s = lax.dot_general(q[:, hh * hd:(hh + 1) * hd], k, (((1,), (1,)), ((), ())),
                            preferred_element_type=F32)
        e = jnp.exp(s - jnp.max(s, axis=-1, keepdims=True))
        denom = jnp.sum(e, axis=-1, keepdims=True)
        o = jnp.dot(e.astype(BF16), v, preferred_element_type=F32) / denom
        a_ref[:, hh * hd:(hh + 1) * hd] = o.astype(BF16)
    d = x_ref.shape[1]
    for c0 in range(0, d, nc):
        cs = slice(c0, c0 + nc)
        o_ref[:, cs] = x_ref[:, cs] + jnp.dot(a_ref[...], wo_ref[:, cs], preferred_element_type=F32)


def _cross(x2d, g, wq, kv3, wo, *, tm, seq, nc):
    m, d = x2d.shape
    n_mem, kvw = kv3.shape[1], kv3.shape[2]
    mw = wq.shape[1]
    tiles_per_batch = seq // tm
    est = (2 * 2 * tm * d * 4 + 2 * n_mem * kvw * 2 + 2 * d * mw * 2 + tm * mw * 2
           + tm * d * 4 + tm * d * 2 + 2 * tm * n_mem * 4 + tm * nc * 4)
    return pl.pallas_call(
        functools.partial(_cross_kernel, nc=nc),
        out_shape=jax.ShapeDtypeStruct((m, d), F32),
        grid=(m // tm,),
        in_specs=[
            pl.BlockSpec((tm, d), lambda i: (i, 0)),
            _resident((1, d), lambda i: (0, 0)),
            _resident((d, mw), lambda i: (0, 0)),
            pl.BlockSpec((1, n_mem, kvw), lambda i: (i // tiles_per_batch, 0, 0)),
            _resident((mw, d), lambda i: (0, 0)),
        ],
        out_specs=pl.BlockSpec((tm, d), lambda i: (i, 0)),
        scratch_shapes=[pltpu.VMEM((tm, mw), BF16)],
        compiler_params=pltpu.CompilerParams(
            dimension_semantics=("arbitrary",),
            vmem_limit_bytes=_vmem_limit(est)),
        name="cross_attn",
    )(x2d, g.reshape(1, d), wq, kv3, wo)


def _ffn_kernel(x_ref, g_ref, wg_ref, wu_ref, wd_ref, gf_ref, o_ref, h_ref, acc_ref, *, final_norm):
    f = pl.program_id(1)

    @pl.when(f == 0)
    def _():
        x = x_ref[...]
        h_ref[...] = _rms_norm_f32(x, g_ref[...]).astype(BF16)
        acc_ref[...] = x

    gate = jnp.dot(h_ref[...], wg_ref[...], preferred_element_type=F32)
    up = jnp.dot(h_ref[...], wu_ref[...], preferred_element_type=F32)
    a = (jax.nn.silu(gate) * up).astype(BF16)
    acc_ref[...] += jnp.dot(a, wd_ref[...], preferred_element_type=F32)

    @pl.when(f == pl.num_programs(1) - 1)
    def _():
        y = acc_ref[...]
        if final_norm:
            y = _rms_norm_f32(y, gf_ref[...])
        o_ref[...] = y


def _ffn(x2d, g, wg, wu, wd, g_final, *, tm, tf, final_norm):
    m, d = x2d.shape
    dff = wg.shape[1]
    est = (2 * 2 * tm * d * 4 + tm * d * 4 + tm * d * 2 + 2 * 3 * d * tf * 2
           + 3 * tm * tf * 4 + tm * d * 4)
    return pl.pallas_call(
        functools.partial(_ffn_kernel, final_norm=final_norm),
        out_shape=jax.ShapeDtypeStruct((m, d), F32),
        grid=(m // tm, dff // tf),
        in_specs=[
            pl.BlockSpec((tm, d), lambda i, f: (i, 0)),
            _resident((1, d), lambda i, f: (0, 0)),
            pl.BlockSpec((d, tf), lambda i, f: (0, f)),
            pl.BlockSpec((d, tf), lambda i, f: (0, f)),
            pl.BlockSpec((tf, d), lambda i, f: (f, 0)),
            _resident((1, d), lambda i, f: (0, 0)),
        ],
        out_specs=pl.BlockSpec((tm, d), lambda i, f: (i, 0)),
        scratch_shapes=[pltpu.VMEM((tm, d), BF16), pltpu.VMEM((tm, d), F32)],
        compiler_params=pltpu.CompilerParams(
            dimension_semantics=("arbitrary", "arbitrary"),
            vmem_limit_bytes=_vmem_limit(est)),
        name="swiglu_ffn",
    )(x2d, g.reshape(1, d), wg, wu, wd, g_final.reshape(1, d))


def _rope_tables(s):
    inv = 1.0 / (ROPE_THETA ** (jnp.arange(0, HEAD_DIM, 2, dtype=F32) / HEAD_DIM))
    ang = jnp.arange(s, dtype=F32)[:, None] * inv[None, :]
    cos, sin = jnp.cos(ang), jnp.sin(ang)
    return jnp.concatenate([cos, cos], axis=-1), jnp.concatenate([-sin, sin], axis=-1)


def _permute_in_proj(w_in):
    aw = N_Q_HEADS * HEAD_DIM
    kw = N_KV_HEADS * HEAD_DIM
    q_end = aw
    kv_end = aw + 2 * kw
    return jnp.concatenate([w_in[:, :q_end], w_in[:, kv_end:], w_in[:, q_end:kv_end]], axis=1)


def kernel(x, mem, g_mix, w_in, sink, conv_w, b_gate, w_attn_out, w_conv_out, w_o,
           g_cross, g_mem, w_cq, w_ckv, w_co, g_ffn, w_gate, w_up, w_down, g_final):
    b, s, d = x.shape
    n_mem = mem.shape[1]
    depth = g_mix.shape[0]
    cos_full, sin_signed = _rope_tables(s)
    x2d = x.reshape(b * s, d)
    mem2d = mem.reshape(b * n_mem, d)
    tm = min(512, s)
    for l in range(depth):
        w_in_l = _permute_in_proj(w_in[l]).astype(BF16)
        z = _norm_matmul(x2d, g_mix[l], w_in_l, tm=min(1024, s), tn=512)
        attn, conv = _local_mixer(z.reshape(b, s, -1), sink[l], conv_w[l], cos_full, sin_signed,
                                  tq=min(512, s))
        x2d = _merge(attn.reshape(b * s, -1), conv.reshape(b * s, -1), z, b_gate[l], x2d,
                     w_attn_out[l].astype(BF16), w_conv_out[l].astype(BF16), w_o[l].astype(BF16),
                     tm=tm, nc=512)
        mem_kv = _norm_matmul(mem2d, g_mem[l], w_ckv[l].astype(BF16),
                              tm=min(1024, b * n_mem), tn=w_ckv.shape[2])
        x2d = _cross(x2d, g_cross[l], w_cq[l].astype(BF16), mem_kv.reshape(b, n_mem, -1),
                     w_co[l].astype(BF16), tm=tm, seq=s, nc=512)
        x2d = _ffn(x2d, g_ffn[l], w_gate[l].astype(BF16), w_up[l].astype(BF16),
                   w_down[l].astype(BF16), g_final, tm=tm, tf=512,
                   final_norm=(l == depth - 1))
    return x2d.reshape(b, s, d)
```

```python
import functools
import math

import jax
import jax.numpy as jnp
from jax import lax
from jax.experimental import pallas as pl
from jax.experimental.pallas import tpu as pltpu

HEAD_DIM = 128
N_Q_HEADS = 8
N_KV_HEADS = 2
Q_GROUP = N_Q_HEADS // N_KV_HEADS
WINDOW_BLOCK = 128
ROPE_THETA = 10000.0
MEM_HEADS = 4
MEM_HEAD_DIM = 128
RMS_EPS = 1e-6
NEG_INF = -1e30
LOG2_E = math.log2(math.e)

V7X_VMEM_BYTES = 64 * 1024 * 1024
BF16_SUBLANE_TILE = 16

F32 = jnp.float32
BF16 = jnp.bfloat16


def _vmem_limit(estimate_bytes):
    return int(min(estimate_bytes + 12 * 1024 * 1024, V7X_VMEM_BYTES - 8 * 1024 * 1024))


def _resident(block_shape, index_map):
    return pl.BlockSpec(block_shape, index_map, pipeline_mode=pl.Buffered(1))


def _rms_norm_f32(x, g):
    y = x * lax.rsqrt(jnp.mean(x * x, axis=-1, keepdims=True) + RMS_EPS)
    return y * g


def _norm_matmul_kernel(x_ref, g_ref, w_ref, o_ref, h_ref):
    @pl.when(pl.program_id(1) == 0)
    def _():
        h_ref[...] = _rms_norm_f32(x_ref[...], g_ref[...]).astype(BF16)

    o_ref[...] = jnp.dot(h_ref[...], w_ref[...], preferred_element_type=F32).astype(o_ref.dtype)


def _norm_matmul(x2d, g, w_bf16, *, tm, tn, col_map=lambda j: j):
    m, d = x2d.shape
    n = w_bf16.shape[1]
    est = 2 * tm * d * 4 + tm * d * 2 + 2 * d * tn * 2 + 2 * tm * tn * 2 + tm * tn * 4
    return pl.pallas_call(
        _norm_matmul_kernel,
        out_shape=jax.ShapeDtypeStruct((m, n), BF16),
        grid=(m // tm, n // tn),
        in_specs=[
            pl.BlockSpec((tm, d), lambda i, j: (i, 0)),
            _resident((1, d), lambda i, j: (0, 0)),
            pl.BlockSpec((d, tn), lambda i, j: (0, col_map(j))),
        ],
        out_specs=pl.BlockSpec((tm, tn), lambda i, j: (i, j)),
        scratch_shapes=[pltpu.VMEM((tm, d), BF16)],
        compiler_params=pltpu.CompilerParams(
            dimension_semantics=("arbitrary", "arbitrary"),
            vmem_limit_bytes=_vmem_limit(est)),
        name="norm_matmul",
    )(x2d, g.reshape(1, d), w_bf16)


def _in_proj_col_map(tn):
    q_blocks = N_Q_HEADS * HEAD_DIM // tn
    kv_blocks = 2 * N_KV_HEADS * HEAD_DIM // tn

    def col_map(j, n_blocks):
        rest_end = n_blocks - kv_blocks
        return jnp.where(j < q_blocks, j, jnp.where(j < rest_end, j + kv_blocks, j - rest_end + q_blocks))

    return col_map


def _rope(t, cos_full, sin_signed):
    return t * cos_full + pltpu.roll(t, HEAD_DIM // 2, axis=1) * sin_signed


def _mixer_kernel(sink_ref, q_ref, kvc_ref, kvp_ref, kvn_ref,
                  cu_ref, cb_ref, cc_ref, cup_ref, ccp_ref, cun_ref, ccn_ref,
                  cw_ref, cos_ref, sin_ref, gla_ref, glc_ref, bg_ref, wa_ref, wc_ref, wo_ref,
                  o_ref,
                  qs_ref, kb_ref, vb_ref, attn_ref, conv_ref, m_ref, *, tq, seq, nc):
    i = pl.program_id(1)
    n_tiles = seq // tq
    nblk = tq // WINDOW_BLOCK
    blk = WINDOW_BLOCK
    hd = HEAD_DIM
    scale = HEAD_DIM ** -0.5
    has_prev = i > 0
    has_next = i < n_tiles - 1

    u = cc_ref[0].astype(F32) * cu_ref[0].astype(F32)
    last = BF16_SUBLANE_TILE - 1
    u_prev = ccp_ref[0, last:last + 1, :].astype(F32) * cup_ref[0, last:last + 1, :].astype(F32)
    u_next = ccn_ref[0, 0:1, :].astype(F32) * cun_ref[0, 0:1, :].astype(F32)
    u_prev = jnp.where(has_prev, u_prev, 0.0)
    u_next = jnp.where(has_next, u_next, 0.0)
    t_idx = lax.broadcasted_iota(jnp.int32, u.shape, 0)
    u_dn = jnp.where(t_idx == 0, u_prev, pltpu.roll(u, 1, axis=0))
    u_up = jnp.where(t_idx == tq - 1, u_next, pltpu.roll(u, tq - 1, axis=0))
    conv = u_dn * cw_ref[0:1, :] + u * cw_ref[1:2, :] + u_up * cw_ref[2:3, :]
    conv_ref[...] = (cb_ref[0].astype(F32) * conv).astype(BF16)

    row0 = pl.multiple_of(i * tq, tq)
    prev0 = pl.multiple_of(jnp.maximum(i * tq - blk, 0), blk)
    next0 = pl.multiple_of(jnp.minimum((i + 1) * tq, seq - blk), blk)
    cos_c, sin_c = cos_ref[pl.ds(row0, tq), :], sin_ref[pl.ds(row0, tq), :]
    cos_p, sin_p = cos_ref[pl.ds(prev0, blk), :], sin_ref[pl.ds(prev0, blk), :]
    cos_n, sin_n = cos_ref[pl.ds(next0, blk), :], sin_ref[pl.ds(next0, blk), :]

    for h in range(N_KV_HEADS):
        ks = slice(h * hd, (h + 1) * hd)
        vs = slice((N_KV_HEADS + h) * hd, (N_KV_HEADS + h + 1) * hd)
        kb_ref[h, 0:blk, :] = _rope(kvp_ref[0, :, ks].astype(F32), cos_p, sin_p).astype(BF16)
        kb_ref[h, blk:blk + tq, :] = _rope(kvc_ref[0, :, ks].astype(F32), cos_c, sin_c).astype(BF16)
        kb_ref[h, blk + tq:, :] = _rope(kvn_ref[0, :, ks].astype(F32), cos_n, sin_n).astype(BF16)
        vb_ref[h, 0:blk, 0:hd] = kvp_ref[0, :, vs]
        vb_ref[h, blk:blk + tq, 0:hd] = kvc_ref[0, :, vs]
        vb_ref[h, blk + tq:, 0:hd] = kvn_ref[0, :, vs]
        vb_ref[h, :, hd:] = jnp.ones((tq + 2 * blk, hd), BF16)

    for qh in range(N_Q_HEADS):
        h, g = divmod(qh, Q_GROUP)
        t = _rope(q_ref[0, :, qh * hd:(qh + 1) * hd].astype(F32), cos_c, sin_c) * (scale * LOG2_E)
        t = t.astype(BF16)
        for j in range(nblk):
            qs_ref[h, j, g * blk:(g + 1) * blk, :] = t[j * blk:(j + 1) * blk, :]

    rows = Q_GROUP * blk
    qp = lax.broadcasted_iota(jnp.int32, (rows, blk), 0) % blk
    kp = lax.broadcasted_iota(jnp.int32, (rows, blk), 1)
    bias_prev = jnp.where(kp >= qp, 0.0, NEG_INF)
    bias_next = jnp.where(kp <= qp, 0.0, NEG_INF)
    bias_prev_edge = jnp.where(has_prev, bias_prev, NEG_INF)
    bias_next_edge = jnp.where(has_next, bias_next, NEG_INF)

    for h in range(N_KV_HEADS):
        sink_b = jnp.concatenate(
            [jnp.full((blk, hd), sink_ref[h * Q_GROUP + g] * LOG2_E, F32) for g in range(Q_GROUP)],
            axis=0)
        for j in range(nblk):
            s = lax.dot_general(qs_ref[h, j], kb_ref[h, j * blk:(j + 3) * blk, :],
                                (((1,), (1,)), ((), ())), preferred_element_type=F32)
            s_prev = s[:, 0:blk] + (bias_prev_edge if j == 0 else bias_prev)
            s_cur = s[:, blk:2 * blk]
            s_next = s[:, 2 * blk:] + (bias_next_edge if j == nblk - 1 else bias_next)
            m = jnp.max(jnp.maximum(jnp.maximum(s_prev, s_cur), s_next), axis=-1, keepdims=True)
            m = jnp.maximum(m, sink_b)
            e = jnp.concatenate(
                [jnp.exp2(s_prev - m), jnp.exp2(s_cur - m), jnp.exp2(s_next - m)], axis=1)
            o_aug = jnp.dot(e.astype(BF16), vb_ref[h, j * blk:(j + 3) * blk, :],
                            preferred_element_type=F32)
            o = o_aug[:, 0:hd] / (o_aug[:, hd:] + jnp.exp2(sink_b - m))
            for g in range(Q_GROUP):
                qh = h * Q_GROUP + g
                attn_ref[j * blk:(j + 1) * blk, qh * hd:(qh + 1) * hd] = (
                    o[g * blk:(g + 1) * blk, :].astype(BF16))

    d = wo_ref.shape[1]
    for c0 in range(0, d, nc):
        cs = slice(c0, c0 + nc)
        yc = jnp.dot(conv_ref[...], wc_ref[:, cs], preferred_element_type=F32)
        ya = jnp.dot(attn_ref[...], wa_ref[:, cs], preferred_element_type=F32)
        ga = jax.nn.sigmoid(gla_ref[0, :, cs].astype(F32) + bg_ref[:, c0:c0 + nc])
        gc = jax.nn.sigmoid(glc_ref[0, :, cs].astype(F32) + bg_ref[:, d + c0:d + c0 + nc])
        m_ref[:, cs] = (ga * ya + gc * yc).astype(BF16)
    for c0 in range(0, d, nc):
        cs = slice(c0, c0 + nc)
        o_ref[0, :, cs] = jnp.dot(m_ref[...], wo_ref[:, cs], preferred_element_type=F32).astype(BF16)


def _mixer(z3, sink, conv_w, cos_full, sin_signed, b_gate, wa, wc, wo, *, tq, nc):
    b, s, _ = z3.shape
    d = wo.shape[1]
    aw = N_Q_HEADS * HEAD_DIM
    kvw = 2 * N_KV_HEADS * HEAD_DIM
    ga_col = 4 * aw // d
    kv_col = (4 * aw + 2 * d) // kvw
    nblk = tq // WINDOW_BLOCK
    n_win = s // WINDOW_BLOCK
    sub = BF16_SUBLANE_TILE
    n_sub = s // sub

    def tile(col):
        return pl.BlockSpec((1, tq, aw), lambda bi, i, col=col: (bi, i, col))

    def halo_prev(col):
        return pl.BlockSpec((1, sub, aw),
                            lambda bi, i, col=col: (bi, jnp.maximum(i * (tq // sub) - 1, 0), col))

    def halo_next(col):
        return pl.BlockSpec((1, sub, aw),
                            lambda bi, i, col=col: (bi, jnp.minimum((i + 1) * (tq // sub), n_sub - 1), col))

    in_specs = [
        pl.BlockSpec(memory_space=pltpu.SMEM),
        tile(0),
        pl.BlockSpec((1, tq, kvw), lambda bi, i: (bi, i, kv_col)),
        pl.BlockSpec((1, WINDOW_BLOCK, kvw),
                     lambda bi, i: (bi, jnp.maximum(i * nblk - 1, 0), kv_col)),
        pl.BlockSpec((1, WINDOW_BLOCK, kvw),
                     lambda bi, i: (bi, jnp.minimum((i + 1) * nblk, n_win - 1), kv_col)),
        tile(1), tile(2), tile(3),
        halo_prev(1), halo_prev(3), halo_next(1), halo_next(3),
        _resident((3, aw), lambda bi, i: (0, 0)),
        _resident((s, HEAD_DIM), lambda bi, i: (0, 0)),
        _resident((s, HEAD_DIM), lambda bi, i: (0, 0)),
        pl.BlockSpec((1, tq, d), lambda bi, i: (bi, i, ga_col)),
        pl.BlockSpec((1, tq, d), lambda bi, i: (bi, i, ga_col + 1)),
        _resident((1, 2 * d), lambda bi, i: (0, 0)),
        _resident((aw, d), lambda bi, i: (0, 0)),
        _resident((aw, d), lambda bi, i: (0, 0)),
        _resident((d, d), lambda bi, i: (0, 0)),
    ]
    band = tq + 2 * WINDOW_BLOCK
    est = (2 * (4 * tq * aw + tq * kvw + 2 * WINDOW_BLOCK * kvw + 4 * sub * aw + 3 * tq * d) * 2
           + 2 * s * HEAD_DIM * 4 + (2 * aw * d + d * d) * 2
           + (N_KV_HEADS * nblk * Q_GROUP * WINDOW_BLOCK * HEAD_DIM + 3 * N_KV_HEADS * band * HEAD_DIM
              + 2 * tq * aw + tq * d) * 2
           + 2 * tq * aw * 4)
    return pl.pallas_call(
        functools.partial(_mixer_kernel, tq=tq, seq=s, nc=nc),
        out_shape=jax.ShapeDtypeStruct((b, s, d), BF16),
        grid=(b, s // tq),
        in_specs=in_specs,
        out_specs=pl.BlockSpec((1, tq, d), lambda bi, i: (bi, i, 0)),
        scratch_shapes=[
            pltpu.VMEM((N_KV_HEADS, nblk, Q_GROUP * WINDOW_BLOCK, HEAD_DIM), BF16),
            pltpu.VMEM((N_KV_HEADS, band, HEAD_DIM), BF16),
            pltpu.VMEM((N_KV_HEADS, band, 2 * HEAD_DIM), BF16),
            pltpu.VMEM((tq, aw), BF16),
            pltpu.VMEM((tq, aw), BF16),
            pltpu.VMEM((tq, d), BF16),
        ],
        compiler_params=pltpu.CompilerParams(
            dimension_semantics=("arbitrary", "arbitrary"),
            vmem_limit_bytes=_vmem_limit(est)),
        name="token_mixer",
    )(sink, z3, z3, z3, z3, z3, z3, z3, z3, z3, z3, z3, conv_w, cos_full, sin_signed,
      z3, z3, b_gate.reshape(1, 2 * d), wa, wc, wo)


def _cross_ffn_kernel(x_ref, dl_ref, gc_ref, wq_ref, kv_ref, wco_ref, gf_ref, wg_ref, wu_ref, wd_ref,
                      gfin_ref, o_ref, h_ref, acc_ref, a_ref, *, final_norm, nc):
    f = pl.program_id(1)

    @pl.when(f == 0)
    def _():
        hd = MEM_HEAD_DIM
        mw = MEM_HEADS * hd
        n_mem = kv_ref.shape[1]
        d = x_ref.shape[1]
        x1 = x_ref[...] + dl_ref[...].astype(F32)
        hc = _rms_norm_f32(x1, gc_ref[...]).astype(BF16)
        q = jnp.dot(hc, wq_ref[...], preferred_element_type=F32) * (MEM_HEAD_DIM ** -0.5 * LOG2_E)
        q = q.astype(BF16)
        ones = jnp.ones((n_mem, hd), BF16)
        for hh in range(MEM_HEADS):
            k = kv_ref[0, :, hh * hd:(hh + 1) * hd]
            v_aug = jnp.concatenate([kv_ref[0, :, mw + hh * hd:mw + (hh + 1) * hd], ones], axis=1)
            s = lax.dot_general(q[:, hh * hd:(hh + 1) * hd], k, (((1,), (1,)), ((), ())),
                                preferred_element_type=F32)
            e = jnp.exp2(s - jnp.max(s, axis=-1, keepdims=True))
            o_aug = jnp.dot(e.astype(BF16), v_aug, preferred_element_type=F32)
            a_ref[:, hh * hd:(hh + 1) * hd] = (o_aug[:, 0:hd] / o_aug[:, hd:]).astype(BF16)
        for c0 in range(0, d, nc):
            cs = slice(c0, c0 + nc)
            acc_ref[:, cs] = (x_ref[:, cs] + dl_ref[:, cs].astype(F32)
                              + jnp.dot(a_ref[...], wco_ref[:, cs], preferred_element_type=F32))
        h_ref[...] = _rms_norm_f32(acc_ref[...], gf_ref[...]).astype(BF16)

    gate = jnp.dot(h_ref[...], wg_ref[...], preferred_element_type=F32)
    up = jnp.dot(h_ref[...], wu_ref[...], preferred_element_type=F32)
    a = (jax.nn.silu(gate) * up).astype(BF16)
    acc_ref[...] += jnp.dot(a, wd_ref[...], preferred_element_type=F32)

    @pl.when(f == pl.num_programs(1) - 1)
    def _():
        y = acc_ref[...]
        if final_norm:
            y = _rms_norm_f32(y, gfin_ref[...])
        o_ref[...] = y


def _cross_ffn(x2d, delta2d, g_cross, wq, kv3, wco, g_ffn, wg, wu, wd, g_final, *,
               tm, tf, seq, final_norm, nc):
    m, d = x2d.shape
    dff = wg.shape[1]
    n_mem, kvw = kv3.shape[1], kv3.shape[2]
    mw = wq.shape[1]
    tiles_per_batch = seq // tm
    est = (2 * tm * d * (4 + 2 + 4) + tm * d * (4 + 2) + tm * mw * 2 + 2 * 3 * d * tf * 2
           + 2 * d * mw * 2 + 2 * n_mem * kvw * 2 + 3 * tm * tf * 4 + 2 * tm * d * 4)
    return pl.pallas_call(
        functools.partial(_cross_ffn_kernel, final_norm=final_norm, nc=nc),
        out_shape=jax.ShapeDtypeStruct((m, d), F32),
        grid=(m // tm, dff // tf),
        in_specs=[
            pl.BlockSpec((tm, d), lambda i, f: (i, 0)),
            pl.BlockSpec((tm, d), lambda i, f: (i, 0)),
            _resident((1, d), lambda i, f: (0, 0)),
            _resident((d, mw), lambda i, f: (0, 0)),
            pl.BlockSpec((1, n_mem, kvw), lambda i, f: (i // tiles_per_batch, 0, 0)),
            _resident((mw, d), lambda i, f: (0, 0)),
            _resident((1, d), lambda i, f: (0, 0)),
            pl.BlockSpec((d, tf), lambda i, f: (0, f)),
            pl.BlockSpec((d, tf), lambda i, f: (0, f)),
            pl.BlockSpec((tf, d), lambda i, f: (f, 0)),
            _resident((1, d), lambda i, f: (0, 0)),
        ],
        out_specs=pl.BlockSpec((tm, d), lambda i, f: (i, 0)),
        scratch_shapes=[pltpu.VMEM((tm, d), BF16), pltpu.VMEM((tm, d), F32), pltpu.VMEM((tm, mw), BF16)],
        compiler_params=pltpu.CompilerParams(
            dimension_semantics=("arbitrary", "arbitrary"),
            vmem_limit_bytes=_vmem_limit(est)),
        name="cross_ffn",
    )(x2d, delta2d, g_cross.reshape(1, d), wq, kv3, wco, g_ffn.reshape(1, d), wg, wu, wd,
      g_final.reshape(1, d))


def _rope_tables(s):
    inv = 1.0 / (ROPE_THETA ** (jnp.arange(0, HEAD_DIM, 2, dtype=F32) / HEAD_DIM))
    ang = jnp.arange(s, dtype=F32)[:, None] * inv[None, :]
    cos, sin = jnp.cos(ang), jnp.sin(ang)
    return jnp.concatenate([cos, cos], axis=-1), jnp.concatenate([-sin, sin], axis=-1)


def kernel(x, mem, g_mix, w_in, sink, conv_w, b_gate, w_attn_out, w_conv_out, w_o,
           g_cross, g_mem, w_cq, w_ckv, w_co, g_ffn, w_gate, w_up, w_down, g_final):
    b, s, d = x.shape
    n_mem = mem.shape[1]
    depth = g_mix.shape[0]
    cos_full, sin_signed = _rope_tables(s)
    x2d = x.reshape(b * s, d)
    mem2d = mem.reshape(b * n_mem, d)
    tn_in = 512
    n_in_blocks = w_in.shape[2] // tn_in
    in_col_map = functools.partial(_in_proj_col_map(tn_in), n_blocks=n_in_blocks)
    for l in range(depth):
        z = _norm_matmul(x2d, g_mix[l], w_in[l].astype(BF16), tm=min(1024, s), tn=tn_in,
                         col_map=in_col_map)
        delta = _mixer(z.reshape(b, s, -1), sink[l], conv_w[l], cos_full, sin_signed, b_gate[l],
                       w_attn_out[l].astype(BF16), w_conv_out[l].astype(BF16), w_o[l].astype(BF16),
                       tq=min(512, s), nc=512)
        mem_kv = _norm_matmul(mem2d, g_mem[l], w_ckv[l].astype(BF16),
                              tm=min(1024, b * n_mem), tn=w_ckv.shape[2])
        x2d = _cross_ffn(x2d, delta.reshape(b * s, d), g_cross[l], w_cq[l].astype(BF16),
                         mem_kv.reshape(b, n_mem, -1), w_co[l].astype(BF16), g_ffn[l],
                         w_gate[l].astype(BF16), w_up[l].astype(BF16), w_down[l].astype(BF16),
                         g_final, tm=min(512, s), tf=512, seq=s,
                         final_norm=(l == depth - 1), nc=512)
    return x2d.reshape(b, s, d)
```

```python
import functools
import math

import jax
import jax.numpy as jnp
from jax import lax
from jax.experimental import pallas as pl
from jax.experimental.pallas import tpu as pltpu

HEAD_DIM = 128
N_Q_HEADS = 8
N_KV_HEADS = 2
Q_GROUP = N_Q_HEADS // N_KV_HEADS
WINDOW_BLOCK = 128
ROPE_THETA = 10000.0
MEM_HEADS = 4
MEM_HEAD_DIM = 128
RMS_EPS = 1e-6
NEG_INF = -1e30
LOG2_E = math.log2(math.e)

V7X_VMEM_BYTES = 64 * 1024 * 1024
BF16_SUBLANE_TILE = 16

F32 = jnp.float32
BF16 = jnp.bfloat16


def _vmem_limit(estimate_bytes):
    return int(min(estimate_bytes + 12 * 1024 * 1024, V7X_VMEM_BYTES - 8 * 1024 * 1024))


def _resident(block_shape, index_map):
    return pl.BlockSpec(block_shape, index_map, pipeline_mode=pl.Buffered(1))


def _rms_norm_f32(x, g):
    y = x * lax.rsqrt(jnp.mean(x * x, axis=-1, keepdims=True) + RMS_EPS)
    return y * g


def _norm_matmul_kernel(x_ref, g_ref, w_ref, o_ref, h_ref, *, tn, col_src):
    h_ref[...] = _rms_norm_f32(x_ref[...], g_ref[...]).astype(BF16)
    for j, src in enumerate(col_src):
        o_ref[:, j * tn:(j + 1) * tn] = jnp.dot(
            h_ref[...], w_ref[:, src * tn:(src + 1) * tn], preferred_element_type=F32).astype(BF16)


def _norm_matmul(x2d, g, w_bf16, *, tm, tn, col_src=None):
    m, d = x2d.shape
    n = w_bf16.shape[1]
    if col_src is None:
        col_src = tuple(range(n // tn))
    est = 2 * tm * d * 4 + tm * d * 2 + d * n * 2 + 2 * tm * n * 2 + 2 * tm * tn * 4
    return pl.pallas_call(
        functools.partial(_norm_matmul_kernel, tn=tn, col_src=col_src),
        out_shape=jax.ShapeDtypeStruct((m, n), BF16),
        grid=(m // tm,),
        in_specs=[
            pl.BlockSpec((tm, d), lambda i: (i, 0)),
            _resident((1, d), lambda i: (0, 0)),
            _resident((d, n), lambda i: (0, 0)),
        ],
        out_specs=pl.BlockSpec((tm, n), lambda i: (i, 0)),
        scratch_shapes=[pltpu.VMEM((tm, d), BF16)],
        compiler_params=pltpu.CompilerParams(
            dimension_semantics=("arbitrary",),
            vmem_limit_bytes=_vmem_limit(est)),
        name="norm_matmul",
    )(x2d, g.reshape(1, d), w_bf16)


def _in_proj_col_src(n_cols, tn):
    q_blocks = N_Q_HEADS * HEAD_DIM // tn
    kv_blocks = 2 * N_KV_HEADS * HEAD_DIM // tn
    blocks = list(range(n_cols // tn))
    return tuple(blocks[:q_blocks] + blocks[q_blocks + kv_blocks:] + blocks[q_blocks:q_blocks + kv_blocks])


def _rope(t, cos_full, sin_signed):
    return t * cos_full + pltpu.roll(t, HEAD_DIM // 2, axis=1) * sin_signed


def _mixer_kernel(sink_ref, q_ref, kvc_ref, kvp_ref, kvn_ref,
                  cu_ref, cb_ref, cc_ref, cup_ref, ccp_ref, cun_ref, ccn_ref,
                  cw_ref, cos_ref, sin_ref, gla_ref, glc_ref, bg_ref, wa_ref, wc_ref, wo_ref,
                  o_ref,
                  qs_ref, kb_ref, vb_ref, attn_ref, conv_ref, m_ref, *, tq, seq, nc):
    i = pl.program_id(1)
    n_tiles = seq // tq
    nblk = tq // WINDOW_BLOCK
    blk = WINDOW_BLOCK
    hd = HEAD_DIM
    scale = HEAD_DIM ** -0.5
    has_prev = i > 0
    has_next = i < n_tiles - 1

    u = cc_ref[0].astype(F32) * cu_ref[0].astype(F32)
    last = BF16_SUBLANE_TILE - 1
    u_prev = ccp_ref[0, last:last + 1, :].astype(F32) * cup_ref[0, last:last + 1, :].astype(F32)
    u_next = ccn_ref[0, 0:1, :].astype(F32) * cun_ref[0, 0:1, :].astype(F32)
    u_prev = jnp.where(has_prev, u_prev, 0.0)
    u_next = jnp.where(has_next, u_next, 0.0)
    t_idx = lax.broadcasted_iota(jnp.int32, u.shape, 0)
    u_dn = jnp.where(t_idx == 0, u_prev, pltpu.roll(u, 1, axis=0))
    u_up = jnp.where(t_idx == tq - 1, u_next, pltpu.roll(u, tq - 1, axis=0))
    conv = u_dn * cw_ref[0:1, :] + u * cw_ref[1:2, :] + u_up * cw_ref[2:3, :]
    conv_ref[...] = (cb_ref[0].astype(F32) * conv).astype(BF16)

    row0 = pl.multiple_of(i * tq, tq)
    prev0 = pl.multiple_of(jnp.maximum(i * tq - blk, 0), blk)
    next0 = pl.multiple_of(jnp.minimum((i + 1) * tq, seq - blk), blk)
    cos_c, sin_c = cos_ref[pl.ds(row0, tq), :], sin_ref[pl.ds(row0, tq), :]
    cos_p, sin_p = cos_ref[pl.ds(prev0, blk), :], sin_ref[pl.ds(prev0, blk), :]
    cos_n, sin_n = cos_ref[pl.ds(next0, blk), :], sin_ref[pl.ds(next0, blk), :]

    for h in range(N_KV_HEADS):
        ks = slice(h * hd, (h + 1) * hd)
        vs = slice((N_KV_HEADS + h) * hd, (N_KV_HEADS + h + 1) * hd)
        kb_ref[h, 0:blk, :] = _rope(kvp_ref[0, :, ks].astype(F32), cos_p, sin_p).astype(BF16)
        kb_ref[h, blk:blk + tq, :] = _rope(kvc_ref[0, :, ks].astype(F32), cos_c, sin_c).astype(BF16)
        kb_ref[h, blk + tq:, :] = _rope(kvn_ref[0, :, ks].astype(F32), cos_n, sin_n).astype(BF16)
        vb_ref[h, 0:blk, 0:hd] = kvp_ref[0, :, vs]
        vb_ref[h, blk:blk + tq, 0:hd] = kvc_ref[0, :, vs]
        vb_ref[h, blk + tq:, 0:hd] = kvn_ref[0, :, vs]
        vb_ref[h, :, hd:] = jnp.ones((tq + 2 * blk, hd), BF16)

    for qh in range(N_Q_HEADS):
        h, g = divmod(qh, Q_GROUP)
        t = _rope(q_ref[0, :, qh * hd:(qh + 1) * hd].astype(F32), cos_c, sin_c) * (scale * LOG2_E)
        t = t.astype(BF16)
        for j in range(nblk):
            qs_ref[h, j, g * blk:(g + 1) * blk, :] = t[j * blk:(j + 1) * blk, :]

    rows = Q_GROUP * blk
    qp = lax.broadcasted_iota(jnp.int32, (rows, blk), 0) % blk
    kp = lax.broadcasted_iota(jnp.int32, (rows, blk), 1)
    bias_prev = jnp.where(kp >= qp, 0.0, NEG_INF)
    bias_next = jnp.where(kp <= qp, 0.0, NEG_INF)
    bias_prev_edge = jnp.where(has_prev, bias_prev, NEG_INF)
    bias_next_edge = jnp.where(has_next, bias_next, NEG_INF)

    for h in range(N_KV_HEADS):
        sink_b = jnp.concatenate(
            [jnp.full((blk, hd), sink_ref[h * Q_GROUP + g] * LOG2_E, F32) for g in range(Q_GROUP)],
            axis=0)
        for j in range(nblk):
            s = lax.dot_general(qs_ref[h, j], kb_ref[h, j * blk:(j + 3) * blk, :],
                                (((1,), (1,)), ((), ())), preferred_element_type=F32)
            s_prev = s[:, 0:blk] + (bias_prev_edge if j == 0 else bias_prev)
            s_cur = s[:, blk:2 * blk]
            s_next = s[:, 2 * blk:] + (bias_next_edge if j == nblk - 1 else bias_next)
            m = jnp.max(jnp.maximum(jnp.maximum(s_prev, s_cur), s_next), axis=-1, keepdims=True)
            m = jnp.maximum(m, sink_b)
            e = jnp.concatenate(
                [jnp.exp2(s_prev - m), jnp.exp2(s_cur - m), jnp.exp2(s_next - m)], axis=1)
            o_aug = jnp.dot(e.astype(BF16), vb_ref[h, j * blk:(j + 3) * blk, :],
                            preferred_element_type=F32)
            o = o_aug[:, 0:hd] / (o_aug[:, hd:] + jnp.exp2(sink_b - m))
            for g in range(Q_GROUP):
                qh = h * Q_GROUP + g
                attn_ref[j * blk:(j + 1) * blk, qh * hd:(qh + 1) * hd] = (
                    o[g * blk:(g + 1) * blk, :].astype(BF16))

    d = wo_ref.shape[1]
    for c0 in range(0, d, nc):
        cs = slice(c0, c0 + nc)
        yc = jnp.dot(conv_ref[...], wc_ref[:, cs], preferred_element_type=F32)
        ya = jnp.dot(attn_ref[...], wa_ref[:, cs], preferred_element_type=F32)
        ga = jax.nn.sigmoid(gla_ref[0, :, cs].astype(F32) + bg_ref[:, c0:c0 + nc])
        gc = jax.nn.sigmoid(glc_ref[0, :, cs].astype(F32) + bg_ref[:, d + c0:d + c0 + nc])
        m_ref[:, cs] = (ga * ya + gc * yc).astype(BF16)
    for c0 in range(0, d, nc):
        cs = slice(c0, c0 + nc)
        o_ref[0, :, cs] = jnp.dot(m_ref[...], wo_ref[:, cs], preferred_element_type=F32).astype(BF16)


def _mixer(z3, sink, conv_w, cos_full, sin_signed, b_gate, wa, wc, wo, *, tq, nc):
    b, s, _ = z3.shape
    d = wo.shape[1]
    aw = N_Q_HEADS * HEAD_DIM
    kvw = 2 * N_KV_HEADS * HEAD_DIM
    ga_col = 4 * aw // d
    kv_col = (4 * aw + 2 * d) // kvw
    nblk = tq // WINDOW_BLOCK
    n_win = s // WINDOW_BLOCK
    sub = BF16_SUBLANE_TILE
    n_sub = s // sub

    def tile(col):
        return pl.BlockSpec((1, tq, aw), lambda bi, i, col=col: (bi, i, col))

    def halo_prev(col):
        return pl.BlockSpec((1, sub, aw),
                            lambda bi, i, col=col: (bi, jnp.maximum(i * (tq // sub) - 1, 0), col))

    def halo_next(col):
        return pl.BlockSpec((1, sub, aw),
                            lambda bi, i, col=col: (bi, jnp.minimum((i + 1) * (tq // sub), n_sub - 1), col))

    in_specs = [
        pl.BlockSpec(memory_space=pltpu.SMEM),
        tile(0),
        pl.BlockSpec((1, tq, kvw), lambda bi, i: (bi, i, kv_col)),
        pl.BlockSpec((1, WINDOW_BLOCK, kvw),
                     lambda bi, i: (bi, jnp.maximum(i * nblk - 1, 0), kv_col)),
        pl.BlockSpec((1, WINDOW_BLOCK, kvw),
                     lambda bi, i: (bi, jnp.minimum((i + 1) * nblk, n_win - 1), kv_col)),
        tile(1), tile(2), tile(3),
        halo_prev(1), halo_prev(3), halo_next(1), halo_next(3),
        _resident((3, aw), lambda bi, i: (0, 0)),
        _resident((s, HEAD_DIM), lambda bi, i: (0, 0)),
        _resident((s, HEAD_DIM), lambda bi, i: (0, 0)),
        pl.BlockSpec((1, tq, d), lambda bi, i: (bi, i, ga_col)),
        pl.BlockSpec((1, tq, d), lambda bi, i: (bi, i, ga_col + 1)),
        _resident((1, 2 * d), lambda bi, i: (0, 0)),
        _resident((aw, d), lambda bi, i: (0, 0)),
        _resident((aw, d), lambda bi, i: (0, 0)),
        _resident((d, d), lambda bi, i: (0, 0)),
    ]
    band = tq + 2 * WINDOW_BLOCK
    est = (2 * (4 * tq * aw + tq * kvw + 2 * WINDOW_BLOCK * kvw + 4 * sub * aw + 3 * tq * d) * 2
           + 2 * s * HEAD_DIM * 4 + (2 * aw * d + d * d) * 2
           + (N_KV_HEADS * nblk * Q_GROUP * WINDOW_BLOCK * HEAD_DIM + 3 * N_KV_HEADS * band * HEAD_DIM
              + 2 * tq * aw + tq * d) * 2
           + 2 * tq * aw * 4)
    return pl.pallas_call(
        functools.partial(_mixer_kernel, tq=tq, seq=s, nc=nc),
        out_shape=jax.ShapeDtypeStruct((b, s, d), BF16),
        grid=(b, s // tq),
        in_specs=in_specs,
        out_specs=pl.BlockSpec((1, tq, d), lambda bi, i: (bi, i, 0)),
        scratch_shapes=[
            pltpu.VMEM((N_KV_HEADS, nblk, Q_GROUP * WINDOW_BLOCK, HEAD_DIM), BF16),
            pltpu.VMEM((N_KV_HEADS, band, HEAD_DIM), BF16),
            pltpu.VMEM((N_KV_HEADS, band, 2 * HEAD_DIM), BF16),
            pltpu.VMEM((tq, aw), BF16),
            pltpu.VMEM((tq, aw), BF16),
            pltpu.VMEM((tq, d), BF16),
        ],
        compiler_params=pltpu.CompilerParams(
            dimension_semantics=("arbitrary", "arbitrary"),
            vmem_limit_bytes=_vmem_limit(est)),
        name="token_mixer",
    )(sink, z3, z3, z3, z3, z3, z3, z3, z3, z3, z3, z3, conv_w, cos_full, sin_signed,
      z3, z3, b_gate.reshape(1, 2 * d), wa, wc, wo)


def _cross_kernel(x_ref, dl_ref, g_ref, wq_ref, kv_ref, wo_ref, o_ref, a_ref, *, nc):
    hd = MEM_HEAD_DIM
    mw = MEM_HEADS * hd
    n_mem = kv_ref.shape[1]
    d = x_ref.shape[1]
    x1 = x_ref[...] + dl_ref[...].astype(F32)
    h = _rms_norm_f32(x1, g_ref[...]).astype(BF16)
    q = jnp.dot(h, wq_ref[...], preferred_element_type=F32) * (MEM_HEAD_DIM ** -0.5 * LOG2_E)
    q = q.astype(BF16)
    ones = jnp.ones((n_mem, hd), BF16)
    for hh in range(MEM_HEADS):
        k = kv_ref[0, :, hh * hd:(hh + 1) * hd]
        v_aug = jnp.concatenate([kv_ref[0, :, mw + hh * hd:mw + (hh + 1) * hd], ones], axis=1)
        s = lax.dot_general(q[:, hh * hd:(hh + 1) * hd], k, (((1,), (1,)), ((), ())),
                            preferred_element_type=F32)
        e = jnp.exp2(s - jnp.max(s, axis=-1, keepdims=True))
        o_aug = jnp.dot(e.astype(BF16), v_aug, preferred_element_type=F32)
        a_ref[:, hh * hd:(hh + 1) * hd] = (o_aug[:, 0:hd] / o_aug[:, hd:]).astype(BF16)
    for c0 in range(0, d, nc):
        cs = slice(c0, c0 + nc)
        o_ref[:, cs] = (x_ref[:, cs] + dl_ref[:, cs].astype(F32)
                        + jnp.dot(a_ref[...], wo_ref[:, cs], preferred_element_type=F32))


def _cross(x2d, delta2d, g, wq, kv3, wo, *, tm, seq, nc):
    m, d = x2d.shape
    n_mem, kvw = kv3.shape[1], kv3.shape[2]
    mw = wq.shape[1]
    tiles_per_batch = seq // tm
    est = (2 * tm * d * (4 + 2 + 4) + 2 * n_mem * kvw * 2 + 2 * d * mw * 2 + tm * mw * 2
           + tm * d * (4 + 2) + 2 * tm * n_mem * 4 + tm * nc * 4)
    return pl.pallas_call(
        functools.partial(_cross_kernel, nc=nc),
        out_shape=jax.ShapeDtypeStruct((m, d), F32),
        grid=(m // tm,),
        in_specs=[
            pl.BlockSpec((tm, d), lambda i: (i, 0)),
            pl.BlockSpec((tm, d), lambda i: (i, 0)),
            _resident((1, d), lambda i: (0, 0)),
            _resident((d, mw), lambda i: (0, 0)),
            pl.BlockSpec((1, n_mem, kvw), lambda i: (i // tiles_per_batch, 0, 0)),
            _resident((mw, d), lambda i: (0, 0)),
        ],
        out_specs=pl.BlockSpec((tm, d), lambda i: (i, 0)),
        scratch_shapes=[pltpu.VMEM((tm, mw), BF16)],
        compiler_params=pltpu.CompilerParams(
            dimension_semantics=("arbitrary",),
            vmem_limit_bytes=_vmem_limit(est)),
        name="cross_attn",
    )(x2d, delta2d, g.reshape(1, d), wq, kv3, wo)


def _ffn_part_kernel(*refs, chunks, first, final_norm):
    if first:
        y_ref, g_ref, wg_ref, wu_ref, wd_ref, gfin_ref, o_ref, h_ref, a_ref = refs
        h_ref[...] = _rms_norm_f32(y_ref[...], g_ref[...]).astype(BF16)
    else:
        y_ref, h_ref, wg_ref, wu_ref, wd_ref, gfin_ref, o_ref, a_ref = refs
    for c0, cw in chunks:
        cs = slice(c0, c0 + cw)
        gate = jnp.dot(h_ref[...], wg_ref[:, cs], preferred_element_type=F32)
        up = jnp.dot(h_ref[...], wu_ref[:, cs], preferred_element_type=F32)
        a_ref[:, cs] = (jax.nn.silu(gate) * up).astype(BF16)
    y = y_ref[...] + jnp.dot(a_ref[...], wd_ref[...], preferred_element_type=F32)
    if final_norm:
        y = _rms_norm_f32(y, gfin_ref[...])
    o_ref[...] = y


def _ffn_part(y2d, h2d, g, wg, wu, wd, g_final, *, part, n_parts, tm, tc, final_norm):
    m, d = y2d.shape
    fp = wg.shape[1] // n_parts
    first = part == 0
    chunks = tuple((c0, min(tc, fp - c0)) for c0 in range(0, fp, tc))
    row = pl.BlockSpec((tm, d), lambda i: (i, 0))
    vec = _resident((1, d), lambda i: (0, 0))
    weights = [
        _resident((d, fp), lambda i: (0, part)),
        _resident((d, fp), lambda i: (0, part)),
        _resident((fp, d), lambda i: (part, 0)),
    ]
    est = 3 * d * fp * 2 + 2 * tm * d * (4 + 4 + 2) + tm * fp * 2 + 2 * tm * tc * 4 + 2 * tm * d * 4
    if first:
        in_specs = [row, vec] + weights + [vec]
        args = (y2d, g.reshape(1, d), wg, wu, wd, g_final.reshape(1, d))
        out_shape = [jax.ShapeDtypeStruct((m, d), F32), jax.ShapeDtypeStruct((m, d), BF16)]
        out_specs = [row, row]
    else:
        in_specs = [row, row] + weights + [vec]
        args = (y2d, h2d, wg, wu, wd, g_final.reshape(1, d))
        out_shape = jax.ShapeDtypeStruct((m, d), F32)
        out_specs = row
    return pl.pallas_call(
        functools.partial(_ffn_part_kernel, chunks=chunks, first=first, final_norm=final_norm),
        out_shape=out_shape,
        grid=(m // tm,),
        in_specs=in_specs,
        out_specs=out_specs,
        scratch_shapes=[pltpu.VMEM((tm, fp), BF16)],
        compiler_params=pltpu.CompilerParams(
            dimension_semantics=("arbitrary",),
            vmem_limit_bytes=_vmem_limit(est)),
        name="ffn_part%d" % part,
    )(*args)


def _rope_tables(s):
    inv = 1.0 / (ROPE_THETA ** (jnp.arange(0, HEAD_DIM, 2, dtype=F32) / HEAD_DIM))
    ang = jnp.arange(s, dtype=F32)[:, None] * inv[None, :]
    cos, sin = jnp.cos(ang), jnp.sin(ang)
    return jnp.concatenate([cos, cos], axis=-1), jnp.concatenate([-sin, sin], axis=-1)


def kernel(x, mem, g_mix, w_in, sink, conv_w, b_gate, w_attn_out, w_conv_out, w_o,
           g_cross, g_mem, w_cq, w_ckv, w_co, g_ffn, w_gate, w_up, w_down, g_final):
    b, s, d = x.shape
    n_mem = mem.shape[1]
    depth = g_mix.shape[0]
    cos_full, sin_signed = _rope_tables(s)
    x2d = x.reshape(b * s, d)
    mem2d = mem.reshape(b * n_mem, d)
    tn_in = 512
    in_col_src = _in_proj_col_src(w_in.shape[2], tn_in)
    for l in range(depth):
        z = _norm_matmul(x2d, g_mix[l], w_in[l].astype(BF16), tm=256, tn=tn_in, col_src=in_col_src)
        delta = _mixer(z.reshape(b, s, -1), sink[l], conv_w[l], cos_full, sin_signed, b_gate[l],
                       w_attn_out[l].astype(BF16), w_conv_out[l].astype(BF16), w_o[l].astype(BF16),
                       tq=min(512, s), nc=512)
        mem_kv = _norm_matmul(mem2d, g_mem[l], w_ckv[l].astype(BF16), tm=256, tn=tn_in)
        x2d = _cross(x2d, delta.reshape(b * s, d), g_cross[l], w_cq[l].astype(BF16),
                     mem_kv.reshape(b, n_mem, -1), w_co[l].astype(BF16), tm=min(512, s), seq=s, nc=512)
        wg, wu, wd = w_gate[l].astype(BF16), w_up[l].astype(BF16), w_down[l].astype(BF16)
        n_parts = 2
        h2d = None
        for part in range(n_parts):
            out = _ffn_part(x2d, h2d, g_ffn[l], wg, wu, wd, g_final, part=part, n_parts=n_parts,
                            tm=256, tc=512,
                            final_norm=(l == depth - 1 and part == n_parts - 1))
            x2d, h2d = out if part == 0 else (out, h2d)
    return x2d.reshape(b, s, d)
```

```python
import functools
import math

import jax
import jax.numpy as jnp
from jax import lax
from jax.experimental import pallas as pl
from jax.experimental.pallas import tpu as pltpu

HEAD_DIM = 128
N_Q_HEADS = 8
N_KV_HEADS = 2
Q_GROUP = N_Q_HEADS // N_KV_HEADS
ATTN_WIDTH = N_Q_HEADS * HEAD_DIM
KV_WIDTH = N_KV_HEADS * HEAD_DIM
WINDOW_BLOCK = 128
ROPE_THETA = 10000.0
MEM_HEADS = 4
MEM_HEAD_DIM = 128
RMS_EPS = 1e-6
NEG_INF = -1e30
LOG2_E = math.log2(math.e)

V7X_VMEM_BYTES = 64 * 1024 * 1024
BF16_SUBLANE_TILE = 16
F32_SUBLANE_TILE = 8

F32 = jnp.float32
BF16 = jnp.bfloat16


def _vmem_limit(estimate_bytes):
    return int(min(estimate_bytes + 12 * 1024 * 1024, V7X_VMEM_BYTES - 8 * 1024 * 1024))


def _resident(block_shape, index_map):
    return pl.BlockSpec(block_shape, index_map, pipeline_mode=pl.Buffered(1))


def _rms_norm_f32(x, g):
    y = x * lax.rsqrt(jnp.mean(x * x, axis=-1, keepdims=True) + RMS_EPS)
    return y * g


def _rope(t, cos_full, sin_signed):
    return t * cos_full + pltpu.roll(t, HEAD_DIM // 2, axis=1) * sin_signed


def _in_proj_kernel(x_ref, g_ref, w_ref, bg_ref, cos_ref, sin_ref, o_ref, h_ref, *, tm, tn, seq, d):
    aw, kw, hd = ATTN_WIDTH, KV_WIDTH, HEAD_DIM
    src_q, src_k, src_cu = 0, aw, aw + 2 * kw
    src_cb, src_cc, src_gl = src_cu + aw, src_cu + 2 * aw, src_cu + 3 * aw
    dst_q, dst_u, dst_cb, dst_kv = 2 * d, 2 * d + aw, 2 * d + 2 * aw, 2 * d + 3 * aw

    h_ref[...] = _rms_norm_f32(x_ref[...], g_ref[...]).astype(BF16)

    def proj(col0, width):
        return jnp.dot(h_ref[...], w_ref[:, col0:col0 + width], preferred_element_type=F32)

    pos0 = pl.multiple_of((pl.program_id(0) * tm) % seq, tm)
    cos = cos_ref[pl.ds(pos0, tm), :]
    sin = sin_ref[pl.ds(pos0, tm), :]
    q_scale = HEAD_DIM ** -0.5 * LOG2_E

    for c0 in range(0, 2 * d, tn):
        o_ref[:, c0:c0 + tn] = jax.nn.sigmoid(
            proj(src_gl + c0, tn) + bg_ref[:, c0:c0 + tn]).astype(BF16)
    for c0 in range(0, aw, tn):
        acc = proj(src_q + c0, tn)
        heads = [_rope(acc[:, k0:k0 + hd], cos, sin) * q_scale for k0 in range(0, tn, hd)]
        o_ref[:, dst_q + c0:dst_q + c0 + tn] = jnp.concatenate(heads, axis=1).astype(BF16)
    for c0 in range(0, aw, tn):
        o_ref[:, dst_u + c0:dst_u + c0 + tn] = (
            proj(src_cc + c0, tn) * proj(src_cu + c0, tn)).astype(BF16)
    for c0 in range(0, aw, tn):
        o_ref[:, dst_cb + c0:dst_cb + c0 + tn] = proj(src_cb + c0, tn).astype(BF16)
    acc = proj(src_k, 2 * kw)
    parts = [_rope(acc[:, k0:k0 + hd], cos, sin) for k0 in range(0, kw, hd)] + [acc[:, kw:]]
    o_ref[:, dst_kv:] = jnp.concatenate(parts, axis=1).astype(BF16)


def _in_proj(x2d, g, w_bf16, b_gate, cos_full, sin_signed, *, tm, tn, seq):
    m, d = x2d.shape
    n_in = w_bf16.shape[1]
    n_out = n_in - ATTN_WIDTH
    est = (2 * tm * d * 4 + tm * d * 2 + d * n_in * 2 + 2 * tm * n_out * 2 + 2 * seq * HEAD_DIM * 4
           + 4 * tm * tn * 4)
    return pl.pallas_call(
        functools.partial(_in_proj_kernel, tm=tm, tn=tn, seq=seq, d=d),
        out_shape=jax.ShapeDtypeStruct((m, n_out), BF16),
        grid=(m // tm,),
        in_specs=[
            pl.BlockSpec((tm, d), lambda i: (i, 0)),
            _resident((1, d), lambda i: (0, 0)),
            _resident((d, n_in), lambda i: (0, 0)),
            _resident((1, 2 * d), lambda i: (0, 0)),
            _resident((seq, HEAD_DIM), lambda i: (0, 0)),
            _resident((seq, HEAD_DIM), lambda i: (0, 0)),
        ],
        out_specs=pl.BlockSpec((tm, n_out), lambda i: (i, 0)),
        scratch_shapes=[pltpu.VMEM((tm, d), BF16)],
        compiler_params=pltpu.CompilerParams(
            dimension_semantics=("arbitrary",),
            vmem_limit_bytes=_vmem_limit(est)),
        name="in_proj",
    )(x2d, g.reshape(1, d), w_bf16, b_gate.reshape(1, 2 * d), cos_full, sin_signed)


def _norm_matmul_kernel(x_ref, g_ref, w_ref, o_ref, *, tn):
    h = _rms_norm_f32(x_ref[...], g_ref[...]).astype(BF16)
    for c0 in range(0, o_ref.shape[1], tn):
        o_ref[:, c0:c0 + tn] = jnp.dot(
            h, w_ref[:, c0:c0 + tn], preferred_element_type=F32).astype(BF16)


def _norm_matmul(x2d, g, w_bf16, *, tm, tn):
    m, d = x2d.shape
    n = w_bf16.shape[1]
    est = 2 * tm * d * 4 + tm * d * 2 + d * n * 2 + 2 * tm * n * 2 + 2 * tm * tn * 4
    return pl.pallas_call(
        functools.partial(_norm_matmul_kernel, tn=tn),
        out_shape=jax.ShapeDtypeStruct((m, n), BF16),
        grid=(m // tm,),
        in_specs=[
            pl.BlockSpec((tm, d), lambda i: (i, 0)),
            _resident((1, d), lambda i: (0, 0)),
            _resident((d, n), lambda i: (0, 0)),
        ],
        out_specs=pl.BlockSpec((tm, n), lambda i: (i, 0)),
        compiler_params=pltpu.CompilerParams(
            dimension_semantics=("arbitrary",),
            vmem_limit_bytes=_vmem_limit(est)),
        name="norm_matmul",
    )(x2d, g.reshape(1, d), w_bf16)


def _mixer_kernel(sink_ref, q_ref, kvc_ref, kvp_ref, kvn_ref, u_ref, cb_ref, up_ref, un_ref,
                  cw_ref, ga_ref, gc_ref, wa_ref, wc_ref, wo_ref,
                  o_ref,
                  kb_ref, vb_ref, attn_ref, conv_ref, m_ref, *, tq, seq, nc, row_groups):
    i = pl.program_id(1)
    n_tiles = seq // tq
    nblk = tq // WINDOW_BLOCK
    blk = WINDOW_BLOCK
    hd = HEAD_DIM
    sub = F32_SUBLANE_TILE
    has_prev = i > 0
    has_next = i < n_tiles - 1

    u = u_ref[0].astype(F32)
    last = BF16_SUBLANE_TILE - 1
    u_prev = jnp.where(has_prev, up_ref[0, last:last + 1, :].astype(F32), 0.0)
    u_next = jnp.where(has_next, un_ref[0, 0:1, :].astype(F32), 0.0)
    rolled_dn = pltpu.roll(u, 1, axis=0)
    rolled_up = pltpu.roll(u, tq - 1, axis=0)
    row = lax.broadcasted_iota(jnp.int32, (sub, u.shape[1]), 0)
    u_dn = jnp.concatenate([jnp.where(row == 0, u_prev, rolled_dn[0:sub]), rolled_dn[sub:]], axis=0)
    u_up = jnp.concatenate([rolled_up[:tq - sub], jnp.where(row == sub - 1, u_next, rolled_up[tq - sub:])],
                           axis=0)
    conv = u_dn * cw_ref[0:1, :] + u * cw_ref[1:2, :] + u_up * cw_ref[2:3, :]
    conv_ref[...] = (cb_ref[0].astype(F32) * conv).astype(BF16)

    for h in range(N_KV_HEADS):
        ks = slice(h * hd, (h + 1) * hd)
        vs = slice((N_KV_HEADS + h) * hd, (N_KV_HEADS + h + 1) * hd)
        kb_ref[h, 0:blk, :] = kvp_ref[0, :, ks]
        kb_ref[h, blk:blk + tq, :] = kvc_ref[0, :, ks]
        kb_ref[h, blk + tq:, :] = kvn_ref[0, :, ks]
        vb_ref[h, 0:blk, 0:hd] = kvp_ref[0, :, vs]
        vb_ref[h, blk:blk + tq, 0:hd] = kvc_ref[0, :, vs]
        vb_ref[h, blk + tq:, 0:hd] = kvn_ref[0, :, vs]
        vb_ref[h, :, hd:] = jnp.ones((tq + 2 * blk, hd), BF16)

    rows = Q_GROUP * blk
    qp = lax.broadcasted_iota(jnp.int32, (rows, blk), 0) % blk
    kp = lax.broadcasted_iota(jnp.int32, (rows, blk), 1)
    bias_prev = jnp.where(kp >= qp, 0.0, NEG_INF)
    bias_next = jnp.where(kp <= qp, 0.0, NEG_INF)
    bias_prev_edge = jnp.where(has_prev, bias_prev, NEG_INF)
    bias_next_edge = jnp.where(has_next, bias_next, NEG_INF)

    def attend(h, j, sink_b):
        q_stack = jnp.concatenate(
            [q_ref[0, j * blk:(j + 1) * blk, (h * Q_GROUP + g) * hd:(h * Q_GROUP + g + 1) * hd]
             for g in range(Q_GROUP)], axis=0)
        s = lax.dot_general(q_stack, kb_ref[h, j * blk:(j + 3) * blk, :],
                            (((1,), (1,)), ((), ())), preferred_element_type=F32)
        s_prev = s[:, 0:blk] + (bias_prev_edge if j == 0 else bias_prev)
        s_cur = s[:, blk:2 * blk]
        s_next = s[:, 2 * blk:] + (bias_next_edge if j == nblk - 1 else bias_next)
        m = jnp.max(jnp.maximum(jnp.maximum(s_prev, s_cur), s_next), axis=-1, keepdims=True)
        m = jnp.maximum(m, sink_b)
        e = jnp.concatenate(
            [jnp.exp2(s_prev - m), jnp.exp2(s_cur - m), jnp.exp2(s_next - m)], axis=1)
        o_aug = jnp.dot(e.astype(BF16), vb_ref[h, j * blk:(j + 3) * blk, :],
                        preferred_element_type=F32)
        o = o_aug[:, 0:hd] / (o_aug[:, hd:] + jnp.exp2(sink_b - m))
        for g in range(Q_GROUP):
            qh = h * Q_GROUP + g
            attn_ref[j * blk:(j + 1) * blk, qh * hd:(qh + 1) * hd] = (
                o[g * blk:(g + 1) * blk, :].astype(BF16))

    def projection_steps(rs):
        d = wo_ref.shape[1]

        def merge(cs):
            yc = jnp.dot(conv_ref[rs, :], wc_ref[:, cs], preferred_element_type=F32)
            ya = jnp.dot(attn_ref[rs, :], wa_ref[:, cs], preferred_element_type=F32)
            m_ref[rs, cs] = (ga_ref[0, rs, cs].astype(F32) * ya
                             + gc_ref[0, rs, cs].astype(F32) * yc).astype(BF16)

        def out(cs):
            o_ref[0, rs, cs] = jnp.dot(m_ref[rs, :], wo_ref[:, cs],
                                       preferred_element_type=F32).astype(BF16)

        chunks = [slice(c0, c0 + nc) for c0 in range(0, d, nc)]
        return ([functools.partial(merge, cs) for cs in chunks]
                + [functools.partial(out, cs) for cs in chunks])

    sink_bs = [jnp.concatenate(
        [jnp.full((blk, hd), sink_ref[h * Q_GROUP + g] * LOG2_E, F32) for g in range(Q_GROUP)], axis=0)
        for h in range(N_KV_HEADS)]
    blocks_per_group = nblk // row_groups

    def attention_steps(r):
        return [functools.partial(attend, h, j, sink_bs[h])
                for j in range(r * blocks_per_group, (r + 1) * blocks_per_group)
                for h in range(N_KV_HEADS)]

    for step in attention_steps(0):
        step()
    for r in range(row_groups):
        proj = projection_steps(slice(r * blocks_per_group * blk, (r + 1) * blocks_per_group * blk))
        att = attention_steps(r + 1) if r + 1 < row_groups else []
        for k in range(max(len(proj), len(att))):
            if k < len(proj):
                proj[k]()
            if k < len(att):
                att[k]()


def _mixer(z3, sink, conv_w, wa, wc, wo, *, tq, nc):
    b, s, _ = z3.shape
    d = wo.shape[1]
    aw = ATTN_WIDTH
    kvw = 2 * KV_WIDTH
    q_col = 2 * d // aw
    kv_col = (2 * d + 3 * aw) // kvw
    nblk = tq // WINDOW_BLOCK
    n_win = s // WINDOW_BLOCK
    sub = BF16_SUBLANE_TILE
    n_sub = s // sub

    def tile(col):
        return pl.BlockSpec((1, tq, aw), lambda bi, i, col=col: (bi, i, col))

    in_specs = [
        pl.BlockSpec(memory_space=pltpu.SMEM),
        tile(q_col),
        pl.BlockSpec((1, tq, kvw), lambda bi, i: (bi, i, kv_col)),
        pl.BlockSpec((1, WINDOW_BLOCK, kvw),
                     lambda bi, i: (bi, jnp.maximum(i * nblk - 1, 0), kv_col)),
        pl.BlockSpec((1, WINDOW_BLOCK, kvw),
                     lambda bi, i: (bi, jnp.minimum((i + 1) * nblk, n_win - 1), kv_col)),
        tile(q_col + 1), tile(q_col + 2),
        pl.BlockSpec((1, sub, aw),
                     lambda bi, i: (bi, jnp.maximum(i * (tq // sub) - 1, 0), q_col + 1)),
        pl.BlockSpec((1, sub, aw),
                     lambda bi, i: (bi, jnp.minimum((i + 1) * (tq // sub), n_sub - 1), q_col + 1)),
        _resident((3, aw), lambda bi, i: (0, 0)),
        pl.BlockSpec((1, tq, d), lambda bi, i: (bi, i, 0)),
        pl.BlockSpec((1, tq, d), lambda bi, i: (bi, i, 1)),
        _resident((aw, d), lambda bi, i: (0, 0)),
        _resident((aw, d), lambda bi, i: (0, 0)),
        _resident((d, d), lambda bi, i: (0, 0)),
    ]
    band = tq + 2 * WINDOW_BLOCK
    est = (2 * (3 * tq * aw + tq * kvw + 2 * WINDOW_BLOCK * kvw + 2 * sub * aw + 3 * tq * d) * 2
           + (2 * aw * d + d * d) * 2
           + (3 * N_KV_HEADS * band * HEAD_DIM + 2 * tq * aw + tq * d) * 2
           + 4 * tq * aw * 4)
    return pl.pallas_call(
        functools.partial(_mixer_kernel, tq=tq, seq=s, nc=nc, row_groups=4 if nblk % 4 == 0 else 1),
        out_shape=jax.ShapeDtypeStruct((b, s, d), BF16),
        grid=(b, s // tq),
        in_specs=in_specs,
        out_specs=pl.BlockSpec((1, tq, d), lambda bi, i: (bi, i, 0)),
        scratch_shapes=[
            pltpu.VMEM((N_KV_HEADS, band, HEAD_DIM), BF16),
            pltpu.VMEM((N_KV_HEADS, band, 2 * HEAD_DIM), BF16),
            pltpu.VMEM((tq, aw), BF16),
            pltpu.VMEM((tq, aw), BF16),
            pltpu.VMEM((tq, d), BF16),
        ],
        compiler_params=pltpu.CompilerParams(
            dimension_semantics=("arbitrary", "arbitrary"),
            vmem_limit_bytes=_vmem_limit(est)),
        name="token_mixer",
    )(sink, z3, z3, z3, z3, z3, z3, z3, z3, conv_w, z3, z3, wa, wc, wo)


def _cross_kernel(x_ref, dl_ref, g_ref, wq_ref, kv_ref, wo_ref, o_ref, a_ref, *, nc):
    hd = MEM_HEAD_DIM
    mw = MEM_HEADS * hd
    n_mem = kv_ref.shape[1]
    d = x_ref.shape[1]
    x1 = x_ref[...] + dl_ref[...].astype(F32)
    h = _rms_norm_f32(x1, g_ref[...]).astype(BF16)
    q = jnp.dot(h, wq_ref[...], preferred_element_type=F32) * (MEM_HEAD_DIM ** -0.5 * LOG2_E)
    q = q.astype(BF16)
    ones = jnp.ones((n_mem, hd), BF16)
    for hh in range(MEM_HEADS):
        k = kv_ref[0, :, hh * hd:(hh + 1) * hd]
        v_aug = jnp.concatenate([kv_ref[0, :, mw + hh * hd:mw + (hh + 1) * hd], ones], axis=1)
        s = lax.dot_general(q[:, hh * hd:(hh + 1) * hd], k, (((1,), (1,)), ((), ())),
                            preferred_element_type=F32)
        e = jnp.exp2(s - jnp.max(s, axis=-1, keepdims=True))
        o_aug = jnp.dot(e.astype(BF16), v_aug, preferred_element_type=F32)
        a_ref[:, hh * hd:(hh + 1) * hd] = (o_aug[:, 0:hd] / o_aug[:, hd:]).astype(BF16)
    for c0 in range(0, d, nc):
        cs = slice(c0, c0 + nc)
        o_ref[:, cs] = (x_ref[:, cs] + dl_ref[:, cs].astype(F32)
                        + jnp.dot(a_ref[...], wo_ref[:, cs], preferred_element_type=F32))


def _cross(x2d, delta2d, g, wq, kv3, wo, *, tm, seq, nc):
    m, d = x2d.shape
    n_mem, kvw = kv3.shape[1], kv3.shape[2]
    mw = wq.shape[1]
    tiles_per_batch = seq // tm
    est = (2 * tm * d * (4 + 2 + 4) + 2 * n_mem * kvw * 2 + 2 * d * mw * 2 + tm * mw * 2
           + tm * d * (4 + 2) + 2 * tm * n_mem * 4 + tm * nc * 4)
    return pl.pallas_call(
        functools.partial(_cross_kernel, nc=nc),
        out_shape=jax.ShapeDtypeStruct((m, d), F32),
        grid=(m // tm,),
        in_specs=[
            pl.BlockSpec((tm, d), lambda i: (i, 0)),
            pl.BlockSpec((tm, d), lambda i: (i, 0)),
            _resident((1, d), lambda i: (0, 0)),
            _resident((d, mw), lambda i: (0, 0)),
            pl.BlockSpec((1, n_mem, kvw), lambda i: (i // tiles_per_batch, 0, 0)),
            _resident((mw, d), lambda i: (0, 0)),
        ],
        out_specs=pl.BlockSpec((tm, d), lambda i: (i, 0)),
        scratch_shapes=[pltpu.VMEM((tm, mw), BF16)],
        compiler_params=pltpu.CompilerParams(
            dimension_semantics=("arbitrary",),
            vmem_limit_bytes=_vmem_limit(est)),
        name="cross_attn",
    )(x2d, delta2d, g.reshape(1, d), wq, kv3, wo)


def _ffn_part_kernel(*refs, chunks, first, final_norm):
    if first:
        y_ref, g_ref, wg_ref, wu_ref, wd_ref, gfin_ref, o_ref, h_ref, a_ref = refs
        h_ref[...] = _rms_norm_f32(y_ref[...], g_ref[...]).astype(BF16)
    else:
        y_ref, h_ref, wg_ref, wu_ref, wd_ref, gfin_ref, o_ref, a_ref = refs
    for c0, cw in chunks:
        cs = slice(c0, c0 + cw)
        gate = jnp.dot(h_ref[...], wg_ref[:, cs], preferred_element_type=F32)
        up = jnp.dot(h_ref[...], wu_ref[:, cs], preferred_element_type=F32)
        a_ref[:, cs] = (jax.nn.silu(gate) * up).astype(BF16)
    y = y_ref[...] + jnp.dot(a_ref[...], wd_ref[...], preferred_element_type=F32)
    if final_norm:
        y = _rms_norm_f32(y, gfin_ref[...])
    o_ref[...] = y


def _ffn_part(y2d, h2d, g, wg, wu, wd, g_final, *, part, n_parts, tm, tc, final_norm):
    m, d = y2d.shape
    fp = wg.shape[1] // n_parts
    first = part == 0
    chunks = tuple((c0, min(tc, fp - c0)) for c0 in range(0, fp, tc))
    row = pl.BlockSpec((tm, d), lambda i: (i, 0))
    vec = _resident((1, d), lambda i: (0, 0))
    weights = [
        _resident((d, fp), lambda i: (0, part)),
        _resident((d, fp), lambda i: (0, part)),
        _resident((fp, d), lambda i: (part, 0)),
    ]
    est = 3 * d * fp * 2 + 2 * tm * d * (4 + 4 + 2) + tm * fp * 2 + 2 * tm * tc * 4 + 2 * tm * d * 4
    if first:
        in_specs = [row, vec] + weights + [vec]
        args = (y2d, g.reshape(1, d), wg, wu, wd, g_final.reshape(1, d))
        out_shape = [jax.ShapeDtypeStruct((m, d), F32), jax.ShapeDtypeStruct((m, d), BF16)]
        out_specs = [row, row]
    else:
        in_specs = [row, row] + weights + [vec]
        args = (y2d, h2d, wg, wu, wd, g_final.reshape(1, d))
        out_shape = jax.ShapeDtypeStruct((m, d), F32)
        out_specs = row
    return pl.pallas_call(
        functools.partial(_ffn_part_kernel, chunks=chunks, first=first, final_norm=final_norm),
        out_shape=out_shape,
        grid=(m // tm,),
        in_specs=in_specs,
        out_specs=out_specs,
        scratch_shapes=[pltpu.VMEM((tm, fp), BF16)],
        compiler_params=pltpu.CompilerParams(
            dimension_semantics=("arbitrary",),
            vmem_limit_bytes=_vmem_limit(est)),
        name="ffn_part%d" % part,
    )(*args)


def _rope_tables(s):
    inv = 1.0 / (ROPE_THETA ** (jnp.arange(0, HEAD_DIM, 2, dtype=F32) / HEAD_DIM))
    ang = jnp.arange(s, dtype=F32)[:, None] * inv[None, :]
    cos, sin = jnp.cos(ang), jnp.sin(ang)
    return jnp.concatenate([cos, cos], axis=-1), jnp.concatenate([-sin, sin], axis=-1)


def kernel(x, mem, g_mix, w_in, sink, conv_w, b_gate, w_attn_out, w_conv_out, w_o,
           g_cross, g_mem, w_cq, w_ckv, w_co, g_ffn, w_gate, w_up, w_down, g_final):
    b, s, d = x.shape
    n_mem = mem.shape[1]
    depth = g_mix.shape[0]
    cos_full, sin_signed = _rope_tables(s)
    x2d = x.reshape(b * s, d)
    mem2d = mem.reshape(b * n_mem, d)
    for l in range(depth):
        z = _in_proj(x2d, g_mix[l], w_in[l].astype(BF16), b_gate[l], cos_full, sin_signed,
                     tm=256, tn=512, seq=s)
        delta = _mixer(z.reshape(b, s, -1), sink[l], conv_w[l],
                       w_attn_out[l].astype(BF16), w_conv_out[l].astype(BF16), w_o[l].astype(BF16),
                       tq=min(512, s), nc=512)
        mem_kv = _norm_matmul(mem2d, g_mem[l], w_ckv[l].astype(BF16), tm=256, tn=512)
        x2d = _cross(x2d, delta.reshape(b * s, d), g_cross[l], w_cq[l].astype(BF16),
                     mem_kv.reshape(b, n_mem, -1), w_co[l].astype(BF16), tm=min(512, s), seq=s, nc=512)
        wg, wu, wd = w_gate[l].astype(BF16), w_up[l].astype(BF16), w_down[l].astype(BF16)
        n_parts = 2
        h2d = None
        for part in range(n_parts):
            out = _ffn_part(x2d, h2d, g_ffn[l], wg, wu, wd, g_final, part=part, n_parts=n_parts,
                            tm=256, tc=512,
                            final_norm=(l == depth - 1 and part == n_parts - 1))
            x2d, h2d = out if part == 0 else (out, h2d)
    return x2d.reshape(b, s, d)
```

```python
import functools
import math

import jax
import jax.numpy as jnp
from jax import lax
from jax.experimental import pallas as pl
from jax.experimental.pallas import tpu as pltpu

HEAD_DIM = 128
N_Q_HEADS = 8
N_KV_HEADS = 2
Q_GROUP = N_Q_HEADS // N_KV_HEADS
ATTN_WIDTH = N_Q_HEADS * HEAD_DIM
KV_WIDTH = N_KV_HEADS * HEAD_DIM
WINDOW_BLOCK = 128
ROPE_THETA = 10000.0
MEM_HEADS = 4
MEM_HEAD_DIM = 128
RMS_EPS = 1e-6
NEG_INF = -1e30
LOG2_E = math.log2(math.e)

V7X_VMEM_BYTES = 64 * 1024 * 1024
BF16_SUBLANE_TILE = 16
F32_SUBLANE_TILE = 8

F32 = jnp.float32
BF16 = jnp.bfloat16


def _vmem_limit(estimate_bytes):
    return int(min(estimate_bytes + 12 * 1024 * 1024, V7X_VMEM_BYTES - 8 * 1024 * 1024))


def _resident(block_shape, index_map):
    return pl.BlockSpec(block_shape, index_map, pipeline_mode=pl.Buffered(1))


def _rms_norm_f32(x, g):
    y = x * lax.rsqrt(jnp.mean(x * x, axis=-1, keepdims=True) + RMS_EPS)
    return y * g


def _rope(t, cos_full, sin_signed):
    return t * cos_full + pltpu.roll(t, HEAD_DIM // 2, axis=1) * sin_signed


def _cast_jobs(arrays, n_steps, step_index):
    in_specs, out_specs, out_shapes = [], [], []
    for a in arrays:
        r, c = a.shape
        rb = BF16_SUBLANE_TILE
        while r % rb or r // rb > n_steps:
            rb += BF16_SUBLANE_TILE
        n_chunks = r // rb

        def imap(*idx, n_chunks=n_chunks):
            return (jnp.minimum(step_index(*idx), n_chunks - 1), 0)

        in_specs.append(pl.BlockSpec((rb, c), imap))
        out_specs.append(pl.BlockSpec((rb, c), imap))
        out_shapes.append(jax.ShapeDtypeStruct((r, c), BF16))
    return in_specs, out_specs, out_shapes


def _run_cast_jobs(src_refs, dst_refs):
    for src, dst in zip(src_refs, dst_refs):
        dst[...] = src[...].astype(BF16)


def _in_proj_kernel(*refs, tn, d, n_cast):
    x_ref, g_ref, w_ref, bg_ref, cos_ref, sin_ref = refs[:6]
    cast_src = refs[6:6 + n_cast]
    o_ref = refs[6 + n_cast]
    cast_dst = refs[7 + n_cast:7 + 2 * n_cast]
    h_ref = refs[7 + 2 * n_cast]
    _run_cast_jobs(cast_src, cast_dst)

    aw, kw, hd = ATTN_WIDTH, KV_WIDTH, HEAD_DIM
    src_q, src_k, src_cu = 0, aw, aw + 2 * kw
    src_cb, src_cc, src_gl = src_cu + aw, src_cu + 2 * aw, src_cu + 3 * aw
    dst_q, dst_u, dst_cb, dst_kv = 2 * d, 2 * d + aw, 2 * d + 2 * aw, 2 * d + 3 * aw

    h_ref[...] = _rms_norm_f32(x_ref[...], g_ref[...]).astype(BF16)

    def proj(col0, width):
        return jnp.dot(h_ref[...], w_ref[:, col0:col0 + width], preferred_element_type=F32)

    cos = cos_ref[...]
    sin = sin_ref[...]
    q_scale = HEAD_DIM ** -0.5 * LOG2_E

    for c0 in range(0, 2 * d, tn):
        o_ref[:, c0:c0 + tn] = jax.nn.sigmoid(
            proj(src_gl + c0, tn) + bg_ref[:, c0:c0 + tn]).astype(BF16)
    for c0 in range(0, aw, tn):
        acc = proj(src_q + c0, tn)
        heads = [_rope(acc[:, k0:k0 + hd], cos, sin) * q_scale for k0 in range(0, tn, hd)]
        o_ref[:, dst_q + c0:dst_q + c0 + tn] = jnp.concatenate(heads, axis=1).astype(BF16)
    for c0 in range(0, aw, tn):
        o_ref[:, dst_u + c0:dst_u + c0 + tn] = (
            proj(src_cc + c0, tn) * proj(src_cu + c0, tn)).astype(BF16)
    for c0 in range(0, aw, tn):
        o_ref[:, dst_cb + c0:dst_cb + c0 + tn] = proj(src_cb + c0, tn).astype(BF16)
    acc = proj(src_k, 2 * kw)
    parts = [_rope(acc[:, k0:k0 + hd], cos, sin) for k0 in range(0, kw, hd)] + [acc[:, kw:]]
    o_ref[:, dst_kv:] = jnp.concatenate(parts, axis=1).astype(BF16)


def _in_proj(x2d, g, w_bf16, b_gate, cos_full, sin_signed, cast_f32, *, tm, tn, seq):
    m, d = x2d.shape
    n_in = w_bf16.shape[1]
    n_out = n_in - ATTN_WIDTH
    tiles_per_seq = seq // tm
    cast_in, cast_out, cast_shapes = _cast_jobs(cast_f32, m // tm, lambda i: i)
    cast_bytes = sum(2 * spec.block_shape[0] * spec.block_shape[1] * (4 + 2) for spec in cast_in)
    est = (2 * tm * d * 4 + tm * d * 2 + d * n_in * 2 + 2 * tm * n_out * 2 + 4 * tm * HEAD_DIM * 4
           + 4 * tm * tn * 4 + cast_bytes)
    outs = pl.pallas_call(
        functools.partial(_in_proj_kernel, tn=tn, d=d, n_cast=len(cast_f32)),
        out_shape=[jax.ShapeDtypeStruct((m, n_out), BF16)] + cast_shapes,
        grid=(m // tm,),
        in_specs=[
            pl.BlockSpec((tm, d), lambda i: (i, 0)),
            _resident((1, d), lambda i: (0, 0)),
            _resident((d, n_in), lambda i: (0, 0)),
            _resident((1, 2 * d), lambda i: (0, 0)),
            pl.BlockSpec((tm, HEAD_DIM), lambda i: (i % tiles_per_seq, 0)),
            pl.BlockSpec((tm, HEAD_DIM), lambda i: (i % tiles_per_seq, 0)),
        ] + cast_in,
        out_specs=[pl.BlockSpec((tm, n_out), lambda i: (i, 0))] + cast_out,
        scratch_shapes=[pltpu.VMEM((tm, d), BF16)],
        compiler_params=pltpu.CompilerParams(
            dimension_semantics=("arbitrary",),
            vmem_limit_bytes=_vmem_limit(est)),
        name="in_proj",
    )(x2d, g.reshape(1, d), w_bf16, b_gate.reshape(1, 2 * d), cos_full, sin_signed, *cast_f32)
    return outs[0], outs[1:]


def _norm_matmul_kernel(x_ref, g_ref, w_ref, o_ref, *, tn):
    h = _rms_norm_f32(x_ref[...], g_ref[...]).astype(BF16)
    for c0 in range(0, o_ref.shape[1], tn):
        o_ref[:, c0:c0 + tn] = jnp.dot(
            h, w_ref[:, c0:c0 + tn], preferred_element_type=F32).astype(BF16)


def _norm_matmul(x2d, g, w_bf16, *, tm, tn):
    m, d = x2d.shape
    n = w_bf16.shape[1]
    est = 2 * tm * d * 4 + tm * d * 2 + d * n * 2 + 2 * tm * n * 2 + 2 * tm * tn * 4
    return pl.pallas_call(
        functools.partial(_norm_matmul_kernel, tn=tn),
        out_shape=jax.ShapeDtypeStruct((m, n), BF16),
        grid=(m // tm,),
        in_specs=[
            pl.BlockSpec((tm, d), lambda i: (i, 0)),
            _resident((1, d), lambda i: (0, 0)),
            _resident((d, n), lambda i: (0, 0)),
        ],
        out_specs=pl.BlockSpec((tm, n), lambda i: (i, 0)),
        compiler_params=pltpu.CompilerParams(
            dimension_semantics=("arbitrary",),
            vmem_limit_bytes=_vmem_limit(est)),
        name="norm_matmul",
    )(x2d, g.reshape(1, d), w_bf16)


def _mixer_kernel(*refs, tq, seq, nc, row_groups, n_cast):
    (sink_ref, q_ref, kvc_ref, kvp_ref, kvn_ref, u_ref, cb_ref, up_ref, un_ref,
     cw_ref, ga_ref, gc_ref, wa_ref, wc_ref, wo_ref) = refs[:15]
    cast_src = refs[15:15 + n_cast]
    o_ref = refs[15 + n_cast]
    cast_dst = refs[16 + n_cast:16 + 2 * n_cast]
    kb_ref, vb_ref, attn_ref, conv_ref, m_ref = refs[16 + 2 * n_cast:]
    _run_cast_jobs(cast_src, cast_dst)

    i = pl.program_id(1)
    n_tiles = seq // tq
    nblk = tq // WINDOW_BLOCK
    blk = WINDOW_BLOCK
    hd = HEAD_DIM
    sub = F32_SUBLANE_TILE
    has_prev = i > 0
    has_next = i < n_tiles - 1

    u = u_ref[0].astype(F32)
    last = BF16_SUBLANE_TILE - 1
    u_prev = jnp.where(has_prev, up_ref[0, last:last + 1, :].astype(F32), 0.0)
    u_next = jnp.where(has_next, un_ref[0, 0:1, :].astype(F32), 0.0)
    rolled_dn = pltpu.roll(u, 1, axis=0)
    rolled_up = pltpu.roll(u, tq - 1, axis=0)
    row = lax.broadcasted_iota(jnp.int32, (sub, u.shape[1]), 0)
    u_dn = jnp.concatenate([jnp.where(row == 0, u_prev, rolled_dn[0:sub]), rolled_dn[sub:]], axis=0)
    u_up = jnp.concatenate([rolled_up[:tq - sub], jnp.where(row == sub - 1, u_next, rolled_up[tq - sub:])],
                           axis=0)
    conv = u_dn * cw_ref[0:1, :] + u * cw_ref[1:2, :] + u_up * cw_ref[2:3, :]
    conv_ref[...] = (cb_ref[0].astype(F32) * conv).astype(BF16)

    for h in range(N_KV_HEADS):
        ks = slice(h * hd, (h + 1) * hd)
        vs = slice((N_KV_HEADS + h) * hd, (N_KV_HEADS + h + 1) * hd)
        kb_ref[h, 0:blk, :] = kvp_ref[0, :, ks]
        kb_ref[h, blk:blk + tq, :] = kvc_ref[0, :, ks]
        kb_ref[h, blk + tq:, :] = kvn_ref[0, :, ks]
        vb_ref[h, 0:blk, 0:hd] = kvp_ref[0, :, vs]
        vb_ref[h, blk:blk + tq, 0:hd] = kvc_ref[0, :, vs]
        vb_ref[h, blk + tq:, 0:hd] = kvn_ref[0, :, vs]
        vb_ref[h, :, hd:] = jnp.ones((tq + 2 * blk, hd), BF16)

    rows = Q_GROUP * blk
    qp = lax.broadcasted_iota(jnp.int32, (rows, blk), 0) % blk
    kp = lax.broadcasted_iota(jnp.int32, (rows, blk), 1)
    bias_prev = jnp.where(kp >= qp, 0.0, NEG_INF)
    bias_next = jnp.where(kp <= qp, 0.0, NEG_INF)
    bias_prev_edge = jnp.where(has_prev, bias_prev, NEG_INF)
    bias_next_edge = jnp.where(has_next, bias_next, NEG_INF)

    def attend(h, j, sink_b):
        q_stack = jnp.concatenate(
            [q_ref[0, j * blk:(j + 1) * blk, (h * Q_GROUP + g) * hd:(h * Q_GROUP + g + 1) * hd]
             for g in range(Q_GROUP)], axis=0)
        s = lax.dot_general(q_stack, kb_ref[h, j * blk:(j + 3) * blk, :],
                            (((1,), (1,)), ((), ())), preferred_element_type=F32)
        s_prev = s[:, 0:blk] + (bias_prev_edge if j == 0 else bias_prev)
        s_cur = s[:, blk:2 * blk]
        s_next = s[:, 2 * blk:] + (bias_next_edge if j == nblk - 1 else bias_next)
        m = jnp.max(jnp.maximum(jnp.maximum(s_prev, s_cur), s_next), axis=-1, keepdims=True)
        m = jnp.maximum(m, sink_b)
        e = jnp.concatenate(
            [jnp.exp2(s_prev - m), jnp.exp2(s_cur - m), jnp.exp2(s_next - m)], axis=1)
        o_aug = jnp.dot(e.astype(BF16), vb_ref[h, j * blk:(j + 3) * blk, :],
                        preferred_element_type=F32)
        o = o_aug[:, 0:hd] / (o_aug[:, hd:] + jnp.exp2(sink_b - m))
        for g in range(Q_GROUP):
            qh = h * Q_GROUP + g
            attn_ref[j * blk:(j + 1) * blk, qh * hd:(qh + 1) * hd] = (
                o[g * blk:(g + 1) * blk, :].astype(BF16))

    def projection_steps(rs):
        d = wo_ref.shape[1]

        def merge(cs):
            yc = jnp.dot(conv_ref[rs, :], wc_ref[:, cs], preferred_element_type=F32)
            ya = jnp.dot(attn_ref[rs, :], wa_ref[:, cs], preferred_element_type=F32)
            m_ref[rs, cs] = (ga_ref[0, rs, cs].astype(F32) * ya
                             + gc_ref[0, rs, cs].astype(F32) * yc).astype(BF16)

        def out(cs):
            o_ref[0, rs, cs] = jnp.dot(m_ref[rs, :], wo_ref[:, cs],
                                       preferred_element_type=F32).astype(BF16)

        chunks = [slice(c0, c0 + nc) for c0 in range(0, d, nc)]
        return ([functools.partial(merge, cs) for cs in chunks]
                + [functools.partial(out, cs) for cs in chunks])

    sink_bs = [jnp.concatenate(
        [jnp.full((blk, hd), sink_ref[h * Q_GROUP + g] * LOG2_E, F32) for g in range(Q_GROUP)], axis=0)
        for h in range(N_KV_HEADS)]
    blocks_per_group = nblk // row_groups

    def attention_steps(r):
        return [functools.partial(attend, h, j, sink_bs[h])
                for j in range(r * blocks_per_group, (r + 1) * blocks_per_group)
                for h in range(N_KV_HEADS)]

    for step in attention_steps(0):
        step()
    for r in range(row_groups):
        proj = projection_steps(slice(r * blocks_per_group * blk, (r + 1) * blocks_per_group * blk))
        att = attention_steps(r + 1) if r + 1 < row_groups else []
        for k in range(max(len(proj), len(att))):
            if k < len(proj):
                proj[k]()
            if k < len(att):
                att[k]()


def _mixer(z3, sink, conv_w, wa, wc, wo, cast_f32, *, tq, nc):
    b, s, _ = z3.shape
    d = wo.shape[1]
    aw = ATTN_WIDTH
    kvw = 2 * KV_WIDTH
    q_col = 2 * d // aw
    kv_col = (2 * d + 3 * aw) // kvw
    nblk = tq // WINDOW_BLOCK
    n_win = s // WINDOW_BLOCK
    sub = BF16_SUBLANE_TILE
    n_sub = s // sub

    def tile(col):
        return pl.BlockSpec((1, tq, aw), lambda bi, i, col=col: (bi, i, col))

    in_specs = [
        pl.BlockSpec(memory_space=pltpu.SMEM),
        tile(q_col),
        pl.BlockSpec((1, tq, kvw), lambda bi, i: (bi, i, kv_col)),
        pl.BlockSpec((1, WINDOW_BLOCK, kvw),
                     lambda bi, i: (bi, jnp.maximum(i * nblk - 1, 0), kv_col)),
        pl.BlockSpec((1, WINDOW_BLOCK, kvw),
                     lambda bi, i: (bi, jnp.minimum((i + 1) * nblk, n_win - 1), kv_col)),
        tile(q_col + 1), tile(q_col + 2),
        pl.BlockSpec((1, sub, aw),
                     lambda bi, i: (bi, jnp.maximum(i * (tq // sub) - 1, 0), q_col + 1)),
        pl.BlockSpec((1, sub, aw),
                     lambda bi, i: (bi, jnp.minimum((i + 1) * (tq // sub), n_sub - 1), q_col + 1)),
        _resident((3, aw), lambda bi, i: (0, 0)),
        pl.BlockSpec((1, tq, d), lambda bi, i: (bi, i, 0)),
        pl.BlockSpec((1, tq, d), lambda bi, i: (bi, i, 1)),
        _resident((aw, d), lambda bi, i: (0, 0)),
        _resident((aw, d), lambda bi, i: (0, 0)),
        _resident((d, d), lambda bi, i: (0, 0)),
    ]
    band = tq + 2 * WINDOW_BLOCK
    n_tiles = s // tq
    cast_in, cast_out, cast_shapes = _cast_jobs(cast_f32, b * n_tiles, lambda bi, i: bi * n_tiles + i)
    cast_bytes = sum(2 * spec.block_shape[0] * spec.block_shape[1] * (4 + 2) for spec in cast_in)
    est = (2 * (3 * tq * aw + tq * kvw + 2 * WINDOW_BLOCK * kvw + 2 * sub * aw + 3 * tq * d) * 2
           + (2 * aw * d + d * d) * 2
           + (3 * N_KV_HEADS * band * HEAD_DIM + 2 * tq * aw + tq * d) * 2
           + 4 * tq * aw * 4 + cast_bytes)
    outs = pl.pallas_call(
        functools.partial(_mixer_kernel, tq=tq, seq=s, nc=nc, row_groups=4 if nblk % 4 == 0 else 1,
                          n_cast=len(cast_f32)),
        out_shape=[jax.ShapeDtypeStruct((b, s, d), BF16)] + cast_shapes,
        grid=(b, n_tiles),
        in_specs=in_specs + cast_in,
        out_specs=[pl.BlockSpec((1, tq, d), lambda bi, i: (bi, i, 0))] + cast_out,
        scratch_shapes=[
            pltpu.VMEM((N_KV_HEADS, band, HEAD_DIM), BF16),
            pltpu.VMEM((N_KV_HEADS, band, 2 * HEAD_DIM), BF16),
            pltpu.VMEM((tq, aw), BF16),
            pltpu.VMEM((tq, aw), BF16),
            pltpu.VMEM((tq, d), BF16),
        ],
        compiler_params=pltpu.CompilerParams(
            dimension_semantics=("arbitrary", "arbitrary"),
            vmem_limit_bytes=_vmem_limit(est)),
        name="token_mixer",
    )(sink, z3, z3, z3, z3, z3, z3, z3, z3, conv_w, z3, z3, wa, wc, wo, *cast_f32)
    return outs[0], outs[1:]


def _cross_kernel(x_ref, dl_ref, g_ref, wq_ref, kv_ref, wo_ref, o_ref, a_ref, *, nc):
    hd = MEM_HEAD_DIM
    mw = MEM_HEADS * hd
    n_mem = kv_ref.shape[1]
    d = x_ref.shape[1]
    x1 = x_ref[...] + dl_ref[...].astype(F32)
    h = _rms_norm_f32(x1, g_ref[...]).astype(BF16)
    q = jnp.dot(h, wq_ref[...], preferred_element_type=F32) * (MEM_HEAD_DIM ** -0.5 * LOG2_E)
    q = q.astype(BF16)
    ones = jnp.ones((n_mem, hd), BF16)
    for hh in range(MEM_HEADS):
        k = kv_ref[0, :, hh * hd:(hh + 1) * hd]
        v_aug = jnp.concatenate([kv_ref[0, :, mw + hh * hd:mw + (hh + 1) * hd], ones], axis=1)
        s = lax.dot_general(q[:, hh * hd:(hh + 1) * hd], k, (((1,), (1,)), ((), ())),
                            preferred_element_type=F32)
        e = jnp.exp2(s - jnp.max(s, axis=-1, keepdims=True))
        o_aug = jnp.dot(e.astype(BF16), v_aug, preferred_element_type=F32)
        a_ref[:, hh * hd:(hh + 1) * hd] = (o_aug[:, 0:hd] / o_aug[:, hd:]).astype(BF16)
    for c0 in range(0, d, nc):
        cs = slice(c0, c0 + nc)
        o_ref[:, cs] = (x_ref[:, cs] + dl_ref[:, cs].astype(F32)
                        + jnp.dot(a_ref[...], wo_ref[:, cs], preferred_element_type=F32))


def _cross(x2d, delta2d, g, wq, kv3, wo, *, tm, seq, nc):
    m, d = x2d.shape
    n_mem, kvw = kv3.shape[1], kv3.shape[2]
    mw = wq.shape[1]
    tiles_per_batch = seq // tm
    est = (2 * tm * d * (4 + 2 + 4) + 2 * n_mem * kvw * 2 + 2 * d * mw * 2 + tm * mw * 2
           + tm * d * (4 + 2) + 2 * tm * n_mem * 4 + tm * nc * 4)
    return pl.pallas_call(
        functools.partial(_cross_kernel, nc=nc),
        out_shape=jax.ShapeDtypeStruct((m, d), F32),
        grid=(m // tm,),
        in_specs=[
            pl.BlockSpec((tm, d), lambda i: (i, 0)),
            pl.BlockSpec((tm, d), lambda i: (i, 0)),
            _resident((1, d), lambda i: (0, 0)),
            _resident((d, mw), lambda i: (0, 0)),
            pl.BlockSpec((1, n_mem, kvw), lambda i: (i // tiles_per_batch, 0, 0)),
            _resident((mw, d), lambda i: (0, 0)),
        ],
        out_specs=pl.BlockSpec((tm, d), lambda i: (i, 0)),
        scratch_shapes=[pltpu.VMEM((tm, mw), BF16)],
        compiler_params=pltpu.CompilerParams(
            dimension_semantics=("arbitrary",),
            vmem_limit_bytes=_vmem_limit(est)),
        name="cross_attn",
    )(x2d, delta2d, g.reshape(1, d), wq, kv3, wo)


def _ffn_part_kernel(*refs, chunks, first, final_norm):
    if first:
        y_ref, g_ref, wg_ref, wu_ref, wd_ref, gfin_ref, o_ref, h_ref, a_ref = refs
        h_ref[...] = _rms_norm_f32(y_ref[...], g_ref[...]).astype(BF16)
    else:
        y_ref, h_ref, wg_ref, wu_ref, wd_ref, gfin_ref, o_ref, a_ref = refs
    for c0, cw in chunks:
        cs = slice(c0, c0 + cw)
        gate = jnp.dot(h_ref[...], wg_ref[:, cs], preferred_element_type=F32)
        up = jnp.dot(h_ref[...], wu_ref[:, cs], preferred_element_type=F32)
        a_ref[:, cs] = (jax.nn.silu(gate) * up).astype(BF16)
    y = y_ref[...] + jnp.dot(a_ref[...], wd_ref[...], preferred_element_type=F32)
    if final_norm:
        y = _rms_norm_f32(y, gfin_ref[...])
    o_ref[...] = y


def _ffn_part(y2d, h2d, g, wg, wu, wd, g_final, *, part, n_parts, tm, tc, final_norm):
    m, d = y2d.shape
    fp = wg.shape[1] // n_parts
    first = part == 0
    chunks = tuple((c0, min(tc, fp - c0)) for c0 in range(0, fp, tc))
    row = pl.BlockSpec((tm, d), lambda i: (i, 0))
    vec = _resident((1, d), lambda i: (0, 0))
    weights = [
        _resident((d, fp), lambda i: (0, part)),
        _resident((d, fp), lambda i: (0, part)),
        _resident((fp, d), lambda i: (part, 0)),
    ]
    est = 3 * d * fp * 2 + 2 * tm * d * (4 + 4 + 2) + tm * fp * 2 + 2 * tm * tc * 4 + 2 * tm * d * 4
    if first:
        in_specs = [row, vec] + weights + [vec]
        args = (y2d, g.reshape(1, d), wg, wu, wd, g_final.reshape(1, d))
        out_shape = [jax.ShapeDtypeStruct((m, d), F32), jax.ShapeDtypeStruct((m, d), BF16)]
        out_specs = [row, row]
    else:
        in_specs = [row, row] + weights + [vec]
        args = (y2d, h2d, wg, wu, wd, g_final.reshape(1, d))
        out_shape = jax.ShapeDtypeStruct((m, d), F32)
        out_specs = row
    return pl.pallas_call(
        functools.partial(_ffn_part_kernel, chunks=chunks, first=first, final_norm=final_norm),
        out_shape=out_shape,
        grid=(m // tm,),
        in_specs=in_specs,
        out_specs=out_specs,
        scratch_shapes=[pltpu.VMEM((tm, fp), BF16)],
        compiler_params=pltpu.CompilerParams(
            dimension_semantics=("arbitrary",),
            vmem_limit_bytes=_vmem_limit(est)),
        name="ffn_part%d" % part,
    )(*args)


def _rope_tables(s):
    inv = 1.0 / (ROPE_THETA ** (jnp.arange(0, HEAD_DIM, 2, dtype=F32) / HEAD_DIM))
    ang = jnp.arange(s, dtype=F32)[:, None] * inv[None, :]
    cos, sin = jnp.cos(ang), jnp.sin(ang)
    return jnp.concatenate([cos, cos], axis=-1), jnp.concatenate([-sin, sin], axis=-1)


def kernel(x, mem, g_mix, w_in, sink, conv_w, b_gate, w_attn_out, w_conv_out, w_o,
           g_cross, g_mem, w_cq, w_ckv, w_co, g_ffn, w_gate, w_up, w_down, g_final):
    b, s, d = x.shape
    n_mem = mem.shape[1]
    depth = g_mix.shape[0]
    cos_full, sin_signed = _rope_tables(s)
    x2d = x.reshape(b * s, d)
    mem2d = mem.reshape(b * n_mem, d)
    for l in range(depth):
        z, (wa, wc, wo, wq, wco, wckv) = _in_proj(
            x2d, g_mix[l], w_in[l].astype(BF16), b_gate[l], cos_full, sin_signed,
            (w_attn_out[l], w_conv_out[l], w_o[l], w_cq[l], w_co[l], w_ckv[l]),
            tm=256, tn=512, seq=s)
        mem_kv = _norm_matmul(mem2d, g_mem[l], wckv, tm=256, tn=512)
        delta, (wg, wu, wd) = _mixer(z.reshape(b, s, -1), sink[l], conv_w[l], wa, wc, wo,
                                     (w_gate[l], w_up[l], w_down[l]), tq=min(512, s), nc=512)
        x2d = _cross(x2d, delta.reshape(b * s, d), g_cross[l], wq, mem_kv.reshape(b, n_mem, -1), wco,
                     tm=min(512, s), seq=s, nc=512)
        n_parts = 2
        h2d = None
        for part in range(n_parts):
            out = _ffn_part(x2d, h2d, g_ffn[l], wg, wu, wd, g_final, part=part, n_parts=n_parts,
                            tm=256, tc=512,
                            final_norm=(l == depth - 1 and part == n_parts - 1))
            x2d, h2d = out if part == 0 else (out, h2d)
    return x2d.reshape(b, s, d)
```

```python
import functools
import math

import jax
import jax.numpy as jnp
from jax import lax
from jax.experimental import pallas as pl
from jax.experimental.pallas import tpu as pltpu

HEAD_DIM = 128
N_Q_HEADS = 8
N_KV_HEADS = 2
Q_GROUP = N_Q_HEADS // N_KV_HEADS
ATTN_WIDTH = N_Q_HEADS * HEAD_DIM
KV_WIDTH = N_KV_HEADS * HEAD_DIM
WINDOW_BLOCK = 128
ROPE_THETA = 10000.0
MEM_HEADS = 4
MEM_HEAD_DIM = 128
RMS_EPS = 1e-6
NEG_INF = -1e30
LOG2_E = math.log2(math.e)

V7X_VMEM_BYTES = 64 * 1024 * 1024
BF16_SUBLANE_TILE = 16
F32_SUBLANE_TILE = 8

F32 = jnp.float32
BF16 = jnp.bfloat16


def _vmem_limit(estimate_bytes):
    return int(min(estimate_bytes + 12 * 1024 * 1024, V7X_VMEM_BYTES - 2 * 1024 * 1024))


def _resident(block_shape, index_map):
    return pl.BlockSpec(block_shape, index_map, pipeline_mode=pl.Buffered(1))


def _rms_norm_f32(x, g):
    y = x * lax.rsqrt(jnp.mean(x * x, axis=-1, keepdims=True) + RMS_EPS)
    return y * g


def _rope(t, cos_full, sin_signed):
    return t * cos_full + pltpu.roll(t, HEAD_DIM // 2, axis=1) * sin_signed


def _cast_jobs(arrays, n_steps, step_index):
    in_specs, out_specs, out_shapes = [], [], []
    for a in arrays:
        r, c = a.shape
        rb = BF16_SUBLANE_TILE
        while r % rb or r // rb > n_steps:
            rb += BF16_SUBLANE_TILE
        n_chunks = r // rb

        def imap(*idx, n_chunks=n_chunks):
            return (jnp.minimum(step_index(*idx), n_chunks - 1), 0)

        in_specs.append(pl.BlockSpec((rb, c), imap))
        out_specs.append(pl.BlockSpec((rb, c), imap))
        out_shapes.append(jax.ShapeDtypeStruct((r, c), BF16))
    return in_specs, out_specs, out_shapes


def _run_cast_jobs(src_refs, dst_refs):
    for src, dst in zip(src_refs, dst_refs):
        dst[...] = src[...].astype(BF16)


def _in_proj_kernel(*refs, tn, d, n_cast):
    x_ref, g_ref, w_ref, bg_ref, cos_ref, sin_ref = refs[:6]
    cast_src = refs[6:6 + n_cast]
    o_ref = refs[6 + n_cast]
    cast_dst = refs[7 + n_cast:7 + 2 * n_cast]
    h_ref = refs[7 + 2 * n_cast]
    _run_cast_jobs(cast_src, cast_dst)

    aw, kw, hd = ATTN_WIDTH, KV_WIDTH, HEAD_DIM
    src_q, src_k, src_cu = 0, aw, aw + 2 * kw
    src_cb, src_cc, src_gl = src_cu + aw, src_cu + 2 * aw, src_cu + 3 * aw
    dst_q, dst_u, dst_cb, dst_kv = 2 * d, 2 * d + aw, 2 * d + 2 * aw, 2 * d + 3 * aw

    h_ref[...] = _rms_norm_f32(x_ref[...], g_ref[...]).astype(BF16)

    def proj(col0, width):
        return jnp.dot(h_ref[...], w_ref[:, col0:col0 + width], preferred_element_type=F32)

    cos = cos_ref[...]
    sin = sin_ref[...]
    q_scale = HEAD_DIM ** -0.5 * LOG2_E

    for c0 in range(0, 2 * d, tn):
        o_ref[:, c0:c0 + tn] = jax.nn.sigmoid(
            proj(src_gl + c0, tn) + bg_ref[:, c0:c0 + tn]).astype(BF16)
    for c0 in range(0, aw, tn):
        acc = proj(src_q + c0, tn)
        heads = [_rope(acc[:, k0:k0 + hd], cos, sin) * q_scale for k0 in range(0, tn, hd)]
        o_ref[:, dst_q + c0:dst_q + c0 + tn] = jnp.concatenate(heads, axis=1).astype(BF16)
    for c0 in range(0, aw, tn):
        o_ref[:, dst_u + c0:dst_u + c0 + tn] = (
            proj(src_cc + c0, tn) * proj(src_cu + c0, tn)).astype(BF16)
    for c0 in range(0, aw, tn):
        o_ref[:, dst_cb + c0:dst_cb + c0 + tn] = proj(src_cb + c0, tn).astype(BF16)
    acc = proj(src_k, 2 * kw)
    parts = [_rope(acc[:, k0:k0 + hd], cos, sin) for k0 in range(0, kw, hd)] + [acc[:, kw:]]
    o_ref[:, dst_kv:] = jnp.concatenate(parts, axis=1).astype(BF16)


def _in_proj(x2d, g, w_bf16, b_gate, cos_full, sin_signed, cast_f32, *, tm, tn, seq):
    m, d = x2d.shape
    n_in = w_bf16.shape[1]
    n_out = n_in - ATTN_WIDTH
    tiles_per_seq = seq // tm
    cast_in, cast_out, cast_shapes = _cast_jobs(cast_f32, m // tm, lambda i: i)
    cast_bytes = sum(2 * spec.block_shape[0] * spec.block_shape[1] * (4 + 2) for spec in cast_in)
    est = (2 * tm * d * 4 + tm * d * 2 + d * n_in * 2 + 2 * tm * n_out * 2 + 4 * tm * HEAD_DIM * 4
           + 4 * tm * tn * 4 + cast_bytes)
    outs = pl.pallas_call(
        functools.partial(_in_proj_kernel, tn=tn, d=d, n_cast=len(cast_f32)),
        out_shape=[jax.ShapeDtypeStruct((m, n_out), BF16)] + cast_shapes,
        grid=(m // tm,),
        in_specs=[
            pl.BlockSpec((tm, d), lambda i: (i, 0)),
            _resident((1, d), lambda i: (0, 0)),
            _resident((d, n_in), lambda i: (0, 0)),
            _resident((1, 2 * d), lambda i: (0, 0)),
            pl.BlockSpec((tm, HEAD_DIM), lambda i: (i % tiles_per_seq, 0)),
            pl.BlockSpec((tm, HEAD_DIM), lambda i: (i % tiles_per_seq, 0)),
        ] + cast_in,
        out_specs=[pl.BlockSpec((tm, n_out), lambda i: (i, 0))] + cast_out,
        scratch_shapes=[pltpu.VMEM((tm, d), BF16)],
        compiler_params=pltpu.CompilerParams(
            dimension_semantics=("arbitrary",),
            vmem_limit_bytes=_vmem_limit(est)),
        name="in_proj",
    )(x2d, g.reshape(1, d), w_bf16, b_gate.reshape(1, 2 * d), cos_full, sin_signed, *cast_f32)
    return outs[0], outs[1:]


def _norm_matmul_kernel(x_ref, g_ref, w_ref, o_ref, *, tn):
    h = _rms_norm_f32(x_ref[...], g_ref[...]).astype(BF16)
    for c0 in range(0, o_ref.shape[1], tn):
        o_ref[:, c0:c0 + tn] = jnp.dot(
            h, w_ref[:, c0:c0 + tn], preferred_element_type=F32).astype(BF16)


def _norm_matmul(x2d, g, w_bf16, *, tm, tn):
    m, d = x2d.shape
    n = w_bf16.shape[1]
    est = 2 * tm * d * 4 + tm * d * 2 + d * n * 2 + 2 * tm * n * 2 + 2 * tm * tn * 4
    return pl.pallas_call(
        functools.partial(_norm_matmul_kernel, tn=tn),
        out_shape=jax.ShapeDtypeStruct((m, n), BF16),
        grid=(m // tm,),
        in_specs=[
            pl.BlockSpec((tm, d), lambda i: (i, 0)),
            _resident((1, d), lambda i: (0, 0)),
            _resident((d, n), lambda i: (0, 0)),
        ],
        out_specs=pl.BlockSpec((tm, n), lambda i: (i, 0)),
        compiler_params=pltpu.CompilerParams(
            dimension_semantics=("arbitrary",),
            vmem_limit_bytes=_vmem_limit(est)),
        name="norm_matmul",
    )(x2d, g.reshape(1, d), w_bf16)


def _mixer_kernel(*refs, tq, seq, nc, row_groups, n_cast):
    (sink_ref, q_ref, kvc_ref, kvp_ref, kvn_ref, u_ref, cb_ref, up_ref, un_ref,
     cw_ref, ga_ref, gc_ref, wa_ref, wc_ref, wo_ref) = refs[:15]
    cast_src = refs[15:15 + n_cast]
    o_ref = refs[15 + n_cast]
    cast_dst = refs[16 + n_cast:16 + 2 * n_cast]
    kb_ref, vb_ref, attn_ref, conv_ref, m_ref = refs[16 + 2 * n_cast:]
    _run_cast_jobs(cast_src, cast_dst)

    i = pl.program_id(1)
    n_tiles = seq // tq
    nblk = tq // WINDOW_BLOCK
    blk = WINDOW_BLOCK
    hd = HEAD_DIM
    sub = F32_SUBLANE_TILE
    has_prev = i > 0
    has_next = i < n_tiles - 1

    u = u_ref[0].astype(F32)
    last = BF16_SUBLANE_TILE - 1
    u_prev = jnp.where(has_prev, up_ref[0, last:last + 1, :].astype(F32), 0.0)
    u_next = jnp.where(has_next, un_ref[0, 0:1, :].astype(F32), 0.0)
    rolled_dn = pltpu.roll(u, 1, axis=0)
    rolled_up = pltpu.roll(u, tq - 1, axis=0)
    row = lax.broadcasted_iota(jnp.int32, (sub, u.shape[1]), 0)
    u_dn = jnp.concatenate([jnp.where(row == 0, u_prev, rolled_dn[0:sub]), rolled_dn[sub:]], axis=0)
    u_up = jnp.concatenate([rolled_up[:tq - sub], jnp.where(row == sub - 1, u_next, rolled_up[tq - sub:])],
                           axis=0)
    conv = u_dn * cw_ref[0:1, :] + u * cw_ref[1:2, :] + u_up * cw_ref[2:3, :]
    conv_ref[...] = (cb_ref[0].astype(F32) * conv).astype(BF16)

    for h in range(N_KV_HEADS):
        ks = slice(h * hd, (h + 1) * hd)
        vs = slice((N_KV_HEADS + h) * hd, (N_KV_HEADS + h + 1) * hd)
        kb_ref[h, 0:blk, :] = kvp_ref[0, :, ks]
        kb_ref[h, blk:blk + tq, :] = kvc_ref[0, :, ks]
        kb_ref[h, blk + tq:, :] = kvn_ref[0, :, ks]
        vb_ref[h, 0:blk, 0:hd] = kvp_ref[0, :, vs]
        vb_ref[h, blk:blk + tq, 0:hd] = kvc_ref[0, :, vs]
        vb_ref[h, blk + tq:, 0:hd] = kvn_ref[0, :, vs]
        vb_ref[h, :, hd:] = jnp.ones((tq + 2 * blk, hd), BF16)

    rows = Q_GROUP * blk
    qp = lax.broadcasted_iota(jnp.int32, (rows, blk), 0) % blk
    kp = lax.broadcasted_iota(jnp.int32, (rows, blk), 1)
    bias_prev = jnp.where(kp >= qp, 0.0, NEG_INF)
    bias_next = jnp.where(kp <= qp, 0.0, NEG_INF)
    bias_prev_edge = jnp.where(has_prev, bias_prev, NEG_INF)
    bias_next_edge = jnp.where(has_next, bias_next, NEG_INF)

    def attend(h, j, sink_b):
        q_stack = jnp.concatenate(
            [q_ref[0, j * blk:(j + 1) * blk, (h * Q_GROUP + g) * hd:(h * Q_GROUP + g + 1) * hd]
             for g in range(Q_GROUP)], axis=0)
        s = lax.dot_general(q_stack, kb_ref[h, j * blk:(j + 3) * blk, :],
                            (((1,), (1,)), ((), ())), preferred_element_type=F32)
        s_prev = s[:, 0:blk] + (bias_prev_edge if j == 0 else bias_prev)
        s_cur = s[:, blk:2 * blk]
        s_next = s[:, 2 * blk:] + (bias_next_edge if j == nblk - 1 else bias_next)
        m = jnp.max(jnp.maximum(jnp.maximum(s_prev, s_cur), s_next), axis=-1, keepdims=True)
        m = jnp.maximum(m, sink_b)
        e = jnp.concatenate(
            [jnp.exp2(s_prev - m), jnp.exp2(s_cur - m), jnp.exp2(s_next - m)], axis=1)
        o_aug = jnp.dot(e.astype(BF16), vb_ref[h, j * blk:(j + 3) * blk, :],
                        preferred_element_type=F32)
        o = o_aug[:, 0:hd] / (o_aug[:, hd:] + jnp.exp2(sink_b - m))
        for g in range(Q_GROUP):
            qh = h * Q_GROUP + g
            attn_ref[j * blk:(j + 1) * blk, qh * hd:(qh + 1) * hd] = (
                o[g * blk:(g + 1) * blk, :].astype(BF16))

    def projection_steps(rs):
        d = wo_ref.shape[1]

        def merge(cs):
            yc = jnp.dot(conv_ref[rs, :], wc_ref[:, cs], preferred_element_type=F32)
            ya = jnp.dot(attn_ref[rs, :], wa_ref[:, cs], preferred_element_type=F32)
            m_ref[rs, cs] = (ga_ref[0, rs, cs].astype(F32) * ya
                             + gc_ref[0, rs, cs].astype(F32) * yc).astype(BF16)

        def out(cs):
            o_ref[0, rs, cs] = jnp.dot(m_ref[rs, :], wo_ref[:, cs],
                                       preferred_element_type=F32).astype(BF16)

        chunks = [slice(c0, c0 + nc) for c0 in range(0, d, nc)]
        return ([functools.partial(merge, cs) for cs in chunks]
                + [functools.partial(out, cs) for cs in chunks])

    sink_bs = [jnp.concatenate(
        [jnp.full((blk, hd), sink_ref[h * Q_GROUP + g] * LOG2_E, F32) for g in range(Q_GROUP)], axis=0)
        for h in range(N_KV_HEADS)]
    blocks_per_group = nblk // row_groups

    def attention_steps(r):
        return [functools.partial(attend, h, j, sink_bs[h])
                for j in range(r * blocks_per_group, (r + 1) * blocks_per_group)
                for h in range(N_KV_HEADS)]

    for step in attention_steps(0):
        step()
    for r in range(row_groups):
        proj = projection_steps(slice(r * blocks_per_group * blk, (r + 1) * blocks_per_group * blk))
        att = attention_steps(r + 1) if r + 1 < row_groups else []
        for k in range(max(len(proj), len(att))):
            if k < len(proj):
                proj[k]()
            if k < len(att):
                att[k]()


def _mixer(z3, sink, conv_w, wa, wc, wo, cast_f32, *, tq, nc):
    b, s, _ = z3.shape
    d = wo.shape[1]
    aw = ATTN_WIDTH
    kvw = 2 * KV_WIDTH
    q_col = 2 * d // aw
    kv_col = (2 * d + 3 * aw) // kvw
    nblk = tq // WINDOW_BLOCK
    n_win = s // WINDOW_BLOCK
    sub = BF16_SUBLANE_TILE
    n_sub = s // sub

    def tile(col):
        return pl.BlockSpec((1, tq, aw), lambda bi, i, col=col: (bi, i, col))

    in_specs = [
        pl.BlockSpec(memory_space=pltpu.SMEM),
        tile(q_col),
        pl.BlockSpec((1, tq, kvw), lambda bi, i: (bi, i, kv_col)),
        pl.BlockSpec((1, WINDOW_BLOCK, kvw),
                     lambda bi, i: (bi, jnp.maximum(i * nblk - 1, 0), kv_col)),
        pl.BlockSpec((1, WINDOW_BLOCK, kvw),
                     lambda bi, i: (bi, jnp.minimum((i + 1) * nblk, n_win - 1), kv_col)),
        tile(q_col + 1), tile(q_col + 2),
        pl.BlockSpec((1, sub, aw),
                     lambda bi, i: (bi, jnp.maximum(i * (tq // sub) - 1, 0), q_col + 1)),
        pl.BlockSpec((1, sub, aw),
                     lambda bi, i: (bi, jnp.minimum((i + 1) * (tq // sub), n_sub - 1), q_col + 1)),
        _resident((3, aw), lambda bi, i: (0, 0)),
        pl.BlockSpec((1, tq, d), lambda bi, i: (bi, i, 0)),
        pl.BlockSpec((1, tq, d), lambda bi, i: (bi, i, 1)),
        _resident((aw, d), lambda bi, i: (0, 0)),
        _resident((aw, d), lambda bi, i: (0, 0)),
        _resident((d, d), lambda bi, i: (0, 0)),
    ]
    band = tq + 2 * WINDOW_BLOCK
    n_tiles = s // tq
    cast_in, cast_out, cast_shapes = _cast_jobs(cast_f32, b * n_tiles, lambda bi, i: bi * n_tiles + i)
    cast_bytes = sum(2 * spec.block_shape[0] * spec.block_shape[1] * (4 + 2) for spec in cast_in)
    est = (2 * (3 * tq * aw + tq * kvw + 2 * WINDOW_BLOCK * kvw + 2 * sub * aw + 3 * tq * d) * 2
           + (2 * aw * d + d * d) * 2
           + (3 * N_KV_HEADS * band * HEAD_DIM + 2 * tq * aw + tq * d) * 2
           + 4 * tq * aw * 4 + cast_bytes)
    outs = pl.pallas_call(
        functools.partial(_mixer_kernel, tq=tq, seq=s, nc=nc, row_groups=4 if nblk % 4 == 0 else 1,
                          n_cast=len(cast_f32)),
        out_shape=[jax.ShapeDtypeStruct((b, s, d), BF16)] + cast_shapes,
        grid=(b, n_tiles),
        in_specs=in_specs + cast_in,
        out_specs=[pl.BlockSpec((1, tq, d), lambda bi, i: (bi, i, 0))] + cast_out,
        scratch_shapes=[
            pltpu.VMEM((N_KV_HEADS, band, HEAD_DIM), BF16),
            pltpu.VMEM((N_KV_HEADS, band, 2 * HEAD_DIM), BF16),
            pltpu.VMEM((tq, aw), BF16),
            pltpu.VMEM((tq, aw), BF16),
            pltpu.VMEM((tq, d), BF16),
        ],
        compiler_params=pltpu.CompilerParams(
            dimension_semantics=("arbitrary", "arbitrary"),
            vmem_limit_bytes=_vmem_limit(est)),
        name="token_mixer",
    )(sink, z3, z3, z3, z3, z3, z3, z3, z3, conv_w, z3, z3, wa, wc, wo, *cast_f32)
    return outs[0], outs[1:]


def _cross_kernel(x_ref, dl_ref, g_ref, wq_ref, kv_ref, wo_ref, o_ref, a_ref, *, nc):
    hd = MEM_HEAD_DIM
    mw = MEM_HEADS * hd
    n_mem = kv_ref.shape[1]
    d = x_ref.shape[1]
    x1 = x_ref[...] + dl_ref[...].astype(F32)
    h = _rms_norm_f32(x1, g_ref[...]).astype(BF16)
    q = jnp.dot(h, wq_ref[...], preferred_element_type=F32) * (MEM_HEAD_DIM ** -0.5 * LOG2_E)
    q = q.astype(BF16)
    ones = jnp.ones((n_mem, hd), BF16)
    for hh in range(MEM_HEADS):
        k = kv_ref[0, :, hh * hd:(hh + 1) * hd]
        v_aug = jnp.concatenate([kv_ref[0, :, mw + hh * hd:mw + (hh + 1) * hd], ones], axis=1)
        s = lax.dot_general(q[:, hh * hd:(hh + 1) * hd], k, (((1,), (1,)), ((), ())),
                            preferred_element_type=F32)
        e = jnp.exp2(s - jnp.max(s, axis=-1, keepdims=True))
        o_aug = jnp.dot(e.astype(BF16), v_aug, preferred_element_type=F32)
        a_ref[:, hh * hd:(hh + 1) * hd] = (o_aug[:, 0:hd] / o_aug[:, hd:]).astype(BF16)
    for c0 in range(0, d, nc):
        cs = slice(c0, c0 + nc)
        o_ref[:, cs] = (x_ref[:, cs] + dl_ref[:, cs].astype(F32)
                        + jnp.dot(a_ref[...], wo_ref[:, cs], preferred_element_type=F32))


def _cross(x2d, delta2d, g, wq, kv3, wo, *, tm, seq, nc):
    m, d = x2d.shape
    n_mem, kvw = kv3.shape[1], kv3.shape[2]
    mw = wq.shape[1]
    tiles_per_batch = seq // tm
    est = (2 * tm * d * (4 + 2 + 4) + 2 * n_mem * kvw * 2 + 2 * d * mw * 2 + tm * mw * 2
           + tm * d * (4 + 2) + 2 * tm * n_mem * 4 + tm * nc * 4)
    return pl.pallas_call(
        functools.partial(_cross_kernel, nc=nc),
        out_shape=jax.ShapeDtypeStruct((m, d), F32),
        grid=(m // tm,),
        in_specs=[
            pl.BlockSpec((tm, d), lambda i: (i, 0)),
            pl.BlockSpec((tm, d), lambda i: (i, 0)),
            _resident((1, d), lambda i: (0, 0)),
            _resident((d, mw), lambda i: (0, 0)),
            pl.BlockSpec((1, n_mem, kvw), lambda i: (i // tiles_per_batch, 0, 0)),
            _resident((mw, d), lambda i: (0, 0)),
        ],
        out_specs=pl.BlockSpec((tm, d), lambda i: (i, 0)),
        scratch_shapes=[pltpu.VMEM((tm, mw), BF16)],
        compiler_params=pltpu.CompilerParams(
            dimension_semantics=("arbitrary",),
            vmem_limit_bytes=_vmem_limit(est)),
        name="cross_attn",
    )(x2d, delta2d, g.reshape(1, d), wq, kv3, wo)


def _ffn_part_kernel(*refs, chunks, first, final_norm):
    if first:
        y_ref, g_ref, wg_ref, wu_ref, wd_ref, gfin_ref, o_ref, h_ref, a_ref = refs
        h_ref[...] = _rms_norm_f32(y_ref[...], g_ref[...]).astype(BF16)
    else:
        y_ref, h_ref, wg_ref, wu_ref, wd_ref, gfin_ref, o_ref, a_ref = refs
    for c0, cw in chunks:
        cs = slice(c0, c0 + cw)
        gate = jnp.dot(h_ref[...], wg_ref[:, cs], preferred_element_type=F32)
        up = jnp.dot(h_ref[...], wu_ref[:, cs], preferred_element_type=F32)
        a_ref[:, cs] = (jax.nn.silu(gate) * up).astype(BF16)
    y = y_ref[...] + jnp.dot(a_ref[...], wd_ref[...], preferred_element_type=F32)
    if final_norm:
        y = _rms_norm_f32(y, gfin_ref[...])
    o_ref[...] = y


def _ffn_part(y2d, h2d, g, wg, wu, wd, g_final, *, part, n_parts, tm, tc, final_norm):
    m, d = y2d.shape
    fp = wg.shape[1] // n_parts
    first = part == 0
    chunks = tuple((c0, min(tc, fp - c0)) for c0 in range(0, fp, tc))
    row = pl.BlockSpec((tm, d), lambda i: (i, 0))
    vec = _resident((1, d), lambda i: (0, 0))
    weights = [
        _resident((d, fp), lambda i: (0, part)),
        _resident((d, fp), lambda i: (0, part)),
        _resident((fp, d), lambda i: (part, 0)),
    ]
    est = 3 * d * fp * 2 + 2 * tm * d * (4 + 4 + 2) + tm * fp * 2 + 2 * tm * tc * 4 + 2 * tm * d * 4
    if first:
        in_specs = [row, vec] + weights + [vec]
        args = (y2d, g.reshape(1, d), wg, wu, wd, g_final.reshape(1, d))
        out_shape = [jax.ShapeDtypeStruct((m, d), F32), jax.ShapeDtypeStruct((m, d), BF16)]
        out_specs = [row, row]
    else:
        in_specs = [row, row] + weights + [vec]
        args = (y2d, h2d, wg, wu, wd, g_final.reshape(1, d))
        out_shape = jax.ShapeDtypeStruct((m, d), F32)
        out_specs = row
    return pl.pallas_call(
        functools.partial(_ffn_part_kernel, chunks=chunks, first=first, final_norm=final_norm),
        out_shape=out_shape,
        grid=(m // tm,),
        in_specs=in_specs,
        out_specs=out_specs,
        scratch_shapes=[pltpu.VMEM((tm, fp), BF16)],
        compiler_params=pltpu.CompilerParams(
            dimension_semantics=("arbitrary",),
            vmem_limit_bytes=_vmem_limit(est)),
        name="ffn_part%d" % part,
    )(*args)


def _rope_tables(s):
    inv = 1.0 / (ROPE_THETA ** (jnp.arange(0, HEAD_DIM, 2, dtype=F32) / HEAD_DIM))
    ang = jnp.arange(s, dtype=F32)[:, None] * inv[None, :]
    cos, sin = jnp.cos(ang), jnp.sin(ang)
    return jnp.concatenate([cos, cos], axis=-1), jnp.concatenate([-sin, sin], axis=-1)


def kernel(x, mem, g_mix, w_in, sink, conv_w, b_gate, w_attn_out, w_conv_out, w_o,
           g_cross, g_mem, w_cq, w_ckv, w_co, g_ffn, w_gate, w_up, w_down, g_final):
    b, s, d = x.shape
    n_mem = mem.shape[1]
    depth = g_mix.shape[0]
    cos_full, sin_signed = _rope_tables(s)
    x2d = x.reshape(b * s, d)
    mem2d = mem.reshape(b * n_mem, d)
    for l in range(depth):
        z, (wa, wc, wo, wq, wco, wckv) = _in_proj(
            x2d, g_mix[l], w_in[l].astype(BF16), b_gate[l], cos_full, sin_signed,
            (w_attn_out[l], w_conv_out[l], w_o[l], w_cq[l], w_co[l], w_ckv[l]),
            tm=256, tn=512, seq=s)
        mem_kv = _norm_matmul(mem2d, g_mem[l], wckv, tm=256, tn=512)
        delta, (wg, wu, wd) = _mixer(z.reshape(b, s, -1), sink[l], conv_w[l], wa, wc, wo,
                                     (w_gate[l], w_up[l], w_down[l]), tq=min(512, s), nc=512)
        x2d = _cross(x2d, delta.reshape(b * s, d), g_cross[l], wq, mem_kv.reshape(b, n_mem, -1), wco,
                     tm=min(1024, s), seq=s, nc=512)
        n_parts = 2
        h2d = None
        for part in range(n_parts):
            out = _ffn_part(x2d, h2d, g_ffn[l], wg, wu, wd, g_final, part=part, n_parts=n_parts,
                            tm=512, tc=512,
                            final_norm=(l == depth - 1 and part == n_parts - 1))
            x2d, h2d = out if part == 0 else (out, h2d)
    return x2d.reshape(b, s, d)
```

```python
import functools
import math

import jax
import jax.numpy as jnp
from jax import lax
from jax.experimental import pallas as pl
from jax.experimental.pallas import tpu as pltpu

HEAD_DIM = 128
N_Q_HEADS = 8
N_KV_HEADS = 2
Q_GROUP = N_Q_HEADS // N_KV_HEADS
ATTN_WIDTH = N_Q_HEADS * HEAD_DIM
KV_WIDTH = N_KV_HEADS * HEAD_DIM
WINDOW_BLOCK = 128
ROPE_THETA = 10000.0
MEM_HEADS = 4
MEM_HEAD_DIM = 128
RMS_EPS = 1e-6
NEG_INF = -1e30
LOG2_E = math.log2(math.e)

V7X_VMEM_BYTES = 64 * 1024 * 1024
VMEM_RESERVE_BYTES = 2 * 1024 * 1024
VMEM_TEMPORARIES_BYTES = 12 * 1024 * 1024
BF16_SUBLANE_TILE = 16
F32_SUBLANE_TILE = 8

F32 = jnp.float32
BF16 = jnp.bfloat16


def _vmem_limit(estimate_bytes):
    return int(min(estimate_bytes + VMEM_TEMPORARIES_BYTES, V7X_VMEM_BYTES - VMEM_RESERVE_BYTES))


def _resident(block_shape, index_map):
    return pl.BlockSpec(block_shape, index_map, pipeline_mode=pl.Buffered(1))


def _rms_norm_f32(x, g):
    y = x * lax.rsqrt(jnp.mean(x * x, axis=-1, keepdims=True) + RMS_EPS)
    return y * g


def _rope(t, cos_full, sin_signed):
    return t * cos_full + pltpu.roll(t, HEAD_DIM // 2, axis=1) * sin_signed


def _cast_jobs(arrays, n_steps, step_index):
    in_specs, out_specs, out_shapes = [], [], []
    for a in arrays:
        r, c = a.shape
        rb = BF16_SUBLANE_TILE
        while r % rb or r // rb > n_steps:
            rb += BF16_SUBLANE_TILE
        n_chunks = r // rb

        def imap(*idx, n_chunks=n_chunks):
            return (jnp.minimum(step_index(*idx), n_chunks - 1), 0)

        in_specs.append(pl.BlockSpec((rb, c), imap))
        out_specs.append(pl.BlockSpec((rb, c), imap))
        out_shapes.append(jax.ShapeDtypeStruct((r, c), BF16))
    return in_specs, out_specs, out_shapes


def _run_cast_jobs(src_refs, dst_refs):
    for src, dst in zip(src_refs, dst_refs):
        dst[...] = src[...].astype(BF16)


def _in_proj_kernel(*refs, tn, d, n_cast):
    x_ref, g_ref, w_ref, bg_ref, cos_ref, sin_ref = refs[:6]
    cast_src = refs[6:6 + n_cast]
    o_ref = refs[6 + n_cast]
    cast_dst = refs[7 + n_cast:7 + 2 * n_cast]
    h_ref = refs[7 + 2 * n_cast]
    _run_cast_jobs(cast_src, cast_dst)

    aw, kw, hd = ATTN_WIDTH, KV_WIDTH, HEAD_DIM
    src_q, src_k, src_cu = 0, aw, aw + 2 * kw
    src_cb, src_cc, src_gl = src_cu + aw, src_cu + 2 * aw, src_cu + 3 * aw
    dst_q, dst_u, dst_cb, dst_kv = 2 * d, 2 * d + aw, 2 * d + 2 * aw, 2 * d + 3 * aw

    h_ref[...] = _rms_norm_f32(x_ref[...], g_ref[...]).astype(BF16)

    def proj(col0, width):
        return jnp.dot(h_ref[...], w_ref[:, col0:col0 + width], preferred_element_type=F32)

    cos = cos_ref[...]
    sin = sin_ref[...]
    q_scale = HEAD_DIM ** -0.5 * LOG2_E

    for c0 in range(0, 2 * d, tn):
        o_ref[:, c0:c0 + tn] = jax.nn.sigmoid(
            proj(src_gl + c0, tn) + bg_ref[:, c0:c0 + tn]).astype(BF16)
    acc = proj(src_k, 2 * kw)
    parts = [_rope(acc[:, k0:k0 + hd], cos, sin) for k0 in range(0, kw, hd)] + [acc[:, kw:]]
    o_ref[:, dst_kv:] = jnp.concatenate(parts, axis=1).astype(BF16)
    for c0 in range(0, aw, tn):
        acc = proj(src_q + c0, tn)
        heads = [_rope(acc[:, k0:k0 + hd], cos, sin) * q_scale for k0 in range(0, tn, hd)]
        o_ref[:, dst_q + c0:dst_q + c0 + tn] = jnp.concatenate(heads, axis=1).astype(BF16)
    for c0 in range(0, aw, tn):
        o_ref[:, dst_u + c0:dst_u + c0 + tn] = (
            proj(src_cc + c0, tn) * proj(src_cu + c0, tn)).astype(BF16)
    for c0 in range(0, aw, tn):
        o_ref[:, dst_cb + c0:dst_cb + c0 + tn] = proj(src_cb + c0, tn).astype(BF16)


def _in_proj(x2d, g, w_bf16, b_gate, cos_full, sin_signed, cast_f32, *, tm, tn, seq):
    m, d = x2d.shape
    n_in = w_bf16.shape[1]
    n_out = n_in - ATTN_WIDTH
    tiles_per_seq = seq // tm
    cast_in, cast_out, cast_shapes = _cast_jobs(cast_f32, m // tm, lambda i: i)
    cast_bytes = sum(2 * spec.block_shape[0] * spec.block_shape[1] * (4 + 2) for spec in cast_in)
    est = (2 * tm * d * 4 + tm * d * 2 + d * n_in * 2 + 2 * tm * n_out * 2 + 4 * tm * HEAD_DIM * 4
           + 4 * tm * tn * 4 + cast_bytes)
    outs = pl.pallas_call(
        functools.partial(_in_proj_kernel, tn=tn, d=d, n_cast=len(cast_f32)),
        out_shape=[jax.ShapeDtypeStruct((m, n_out), BF16)] + cast_shapes,
        grid=(m // tm,),
        in_specs=[
            pl.BlockSpec((tm, d), lambda i: (i, 0)),
            _resident((1, d), lambda i: (0, 0)),
            _resident((d, n_in), lambda i: (0, 0)),
            _resident((1, 2 * d), lambda i: (0, 0)),
            pl.BlockSpec((tm, HEAD_DIM), lambda i: (i % tiles_per_seq, 0)),
            pl.BlockSpec((tm, HEAD_DIM), lambda i: (i % tiles_per_seq, 0)),
        ] + cast_in,
        out_specs=[pl.BlockSpec((tm, n_out), lambda i: (i, 0))] + cast_out,
        scratch_shapes=[pltpu.VMEM((tm, d), BF16)],
        compiler_params=pltpu.CompilerParams(
            dimension_semantics=("arbitrary",),
            vmem_limit_bytes=_vmem_limit(est)),
        name="in_proj",
    )(x2d, g.reshape(1, d), w_bf16, b_gate.reshape(1, 2 * d), cos_full, sin_signed, *cast_f32)
    return outs[0], outs[1:]


def _norm_matmul_kernel(x_ref, g_ref, w_ref, o_ref, *, tn):
    h = _rms_norm_f32(x_ref[...], g_ref[...]).astype(BF16)
    for c0 in range(0, o_ref.shape[1], tn):
        o_ref[:, c0:c0 + tn] = jnp.dot(
            h, w_ref[:, c0:c0 + tn], preferred_element_type=F32).astype(BF16)


def _norm_matmul(x2d, g, w_bf16, *, tm, tn):
    m, d = x2d.shape
    n = w_bf16.shape[1]
    est = 2 * tm * d * 4 + tm * d * 2 + d * n * 2 + 2 * tm * n * 2 + 2 * tm * tn * 4
    return pl.pallas_call(
        functools.partial(_norm_matmul_kernel, tn=tn),
        out_shape=jax.ShapeDtypeStruct((m, n), BF16),
        grid=(m // tm,),
        in_specs=[
            pl.BlockSpec((tm, d), lambda i: (i, 0)),
            _resident((1, d), lambda i: (0, 0)),
            _resident((d, n), lambda i: (0, 0)),
        ],
        out_specs=pl.BlockSpec((tm, n), lambda i: (i, 0)),
        compiler_params=pltpu.CompilerParams(
            dimension_semantics=("arbitrary",),
            vmem_limit_bytes=_vmem_limit(est)),
        name="norm_matmul",
    )(x2d, g.reshape(1, d), w_bf16)


def _mixer_kernel(*refs, tq, seq, nc, row_groups, n_cast):
    (sink_ref, q_ref, kvc_ref, kvp_ref, kvn_ref, u_ref, cb_ref, up_ref, un_ref,
     cw_ref, ga_ref, gc_ref, wa_ref, wc_ref, wo_ref) = refs[:15]
    cast_src = refs[15:15 + n_cast]
    o_ref = refs[15 + n_cast]
    cast_dst = refs[16 + n_cast:16 + 2 * n_cast]
    kb_ref, vb_ref, attn_ref, conv_ref, m_ref = refs[16 + 2 * n_cast:]
    _run_cast_jobs(cast_src, cast_dst)

    i = pl.program_id(1)
    n_tiles = seq // tq
    nblk = tq // WINDOW_BLOCK
    blk = WINDOW_BLOCK
    hd = HEAD_DIM
    sub = F32_SUBLANE_TILE
    has_prev = i > 0
    has_next = i < n_tiles - 1

    u = u_ref[0].astype(F32)
    last = BF16_SUBLANE_TILE - 1
    u_prev = jnp.where(has_prev, up_ref[0, last:last + 1, :].astype(F32), 0.0)
    u_next = jnp.where(has_next, un_ref[0, 0:1, :].astype(F32), 0.0)
    rolled_dn = pltpu.roll(u, 1, axis=0)
    rolled_up = pltpu.roll(u, tq - 1, axis=0)
    row = lax.broadcasted_iota(jnp.int32, (sub, u.shape[1]), 0)
    u_dn = jnp.concatenate([jnp.where(row == 0, u_prev, rolled_dn[0:sub]), rolled_dn[sub:]], axis=0)
    u_up = jnp.concatenate([rolled_up[:tq - sub], jnp.where(row == sub - 1, u_next, rolled_up[tq - sub:])],
                           axis=0)
    conv = u_dn * cw_ref[0:1, :] + u * cw_ref[1:2, :] + u_up * cw_ref[2:3, :]
    conv_ref[...] = (cb_ref[0].astype(F32) * conv).astype(BF16)

    for h in range(N_KV_HEADS):
        ks = slice(h * hd, (h + 1) * hd)
        vs = slice((N_KV_HEADS + h) * hd, (N_KV_HEADS + h + 1) * hd)
        kb_ref[h, 0:blk, :] = kvp_ref[0, :, ks]
        kb_ref[h, blk:blk + tq, :] = kvc_ref[0, :, ks]
        kb_ref[h, blk + tq:, :] = kvn_ref[0, :, ks]
        vb_ref[h, 0:blk, 0:hd] = kvp_ref[0, :, vs]
        vb_ref[h, blk:blk + tq, 0:hd] = kvc_ref[0, :, vs]
        vb_ref[h, blk + tq:, 0:hd] = kvn_ref[0, :, vs]
        vb_ref[h, :, hd:] = jnp.ones((tq + 2 * blk, hd), BF16)

    rows = Q_GROUP * blk
    qp = lax.broadcasted_iota(jnp.int32, (rows, blk), 0) % blk
    kp = lax.broadcasted_iota(jnp.int32, (rows, blk), 1)
    bias_prev = jnp.where(kp >= qp, 0.0, NEG_INF)
    bias_next = jnp.where(kp <= qp, 0.0, NEG_INF)
    bias_prev_edge = jnp.where(has_prev, bias_prev, NEG_INF)
    bias_next_edge = jnp.where(has_next, bias_next, NEG_INF)

    def attend(h, j, sink_b):
        q_stack = jnp.concatenate(
            [q_ref[0, j * blk:(j + 1) * blk, (h * Q_GROUP + g) * hd:(h * Q_GROUP + g + 1) * hd]
             for g in range(Q_GROUP)], axis=0)
        s = lax.dot_general(q_stack, kb_ref[h, j * blk:(j + 3) * blk, :],
                            (((1,), (1,)), ((), ())), preferred_element_type=F32)
        s_prev = s[:, 0:blk] + (bias_prev_edge if j == 0 else bias_prev)
        s_cur = s[:, blk:2 * blk]
        s_next = s[:, 2 * blk:] + (bias_next_edge if j == nblk - 1 else bias_next)
        m = jnp.max(jnp.maximum(jnp.maximum(s_prev, s_cur), s_next), axis=-1, keepdims=True)
        m = jnp.maximum(m, sink_b)
        e = jnp.concatenate(
            [jnp.exp2(s_prev - m), jnp.exp2(s_cur - m), jnp.exp2(s_next - m)], axis=1)
        o_aug = jnp.dot(e.astype(BF16), vb_ref[h, j * blk:(j + 3) * blk, :],
                        preferred_element_type=F32)
        o = o_aug[:, 0:hd] / (o_aug[:, hd:] + jnp.exp2(sink_b - m))
        for g in range(Q_GROUP):
            qh = h * Q_GROUP + g
            attn_ref[j * blk:(j + 1) * blk, qh * hd:(qh + 1) * hd] = (
                o[g * blk:(g + 1) * blk, :].astype(BF16))

    def projection_steps(rs):
        d = wo_ref.shape[1]

        def merge(cs):
            yc = jnp.dot(conv_ref[rs, :], wc_ref[:, cs], preferred_element_type=F32)
            ya = jnp.dot(attn_ref[rs, :], wa_ref[:, cs], preferred_element_type=F32)
            m_ref[rs, cs] = (ga_ref[0, rs, cs].astype(F32) * ya
                             + gc_ref[0, rs, cs].astype(F32) * yc).astype(BF16)

        def out(cs):
            o_ref[0, rs, cs] = jnp.dot(m_ref[rs, :], wo_ref[:, cs],
                                       preferred_element_type=F32).astype(BF16)

        chunks = [slice(c0, c0 + nc) for c0 in range(0, d, nc)]
        return ([functools.partial(merge, cs) for cs in chunks]
                + [functools.partial(out, cs) for cs in chunks])

    sink_bs = [jnp.concatenate(
        [jnp.full((blk, hd), sink_ref[h * Q_GROUP + g] * LOG2_E, F32) for g in range(Q_GROUP)], axis=0)
        for h in range(N_KV_HEADS)]
    blocks_per_group = nblk // row_groups

    def attention_steps(r):
        return [functools.partial(attend, h, j, sink_bs[h])
                for j in range(r * blocks_per_group, (r + 1) * blocks_per_group)
                for h in range(N_KV_HEADS)]

    for step in attention_steps(0):
        step()
    for r in range(row_groups):
        proj = projection_steps(slice(r * blocks_per_group * blk, (r + 1) * blocks_per_group * blk))
        att = attention_steps(r + 1) if r + 1 < row_groups else []
        for k in range(max(len(proj), len(att))):
            if k < len(proj):
                proj[k]()
            if k < len(att):
                att[k]()


def _mixer(z3, sink, conv_w, wa, wc, wo, cast_f32, *, tq, nc):
    b, s, _ = z3.shape
    d = wo.shape[1]
    aw = ATTN_WIDTH
    kvw = 2 * KV_WIDTH
    q_col = 2 * d // aw
    kv_col = (2 * d + 3 * aw) // kvw
    nblk = tq // WINDOW_BLOCK
    n_win = s // WINDOW_BLOCK
    sub = BF16_SUBLANE_TILE
    n_sub = s // sub

    def tile(col):
        return pl.BlockSpec((1, tq, aw), lambda bi, i, col=col: (bi, i, col))

    in_specs = [
        pl.BlockSpec(memory_space=pltpu.SMEM),
        tile(q_col),
        pl.BlockSpec((1, tq, kvw), lambda bi, i: (bi, i, kv_col)),
        pl.BlockSpec((1, WINDOW_BLOCK, kvw),
                     lambda bi, i: (bi, jnp.maximum(i * nblk - 1, 0), kv_col)),
        pl.BlockSpec((1, WINDOW_BLOCK, kvw),
                     lambda bi, i: (bi, jnp.minimum((i + 1) * nblk, n_win - 1), kv_col)),
        tile(q_col + 1), tile(q_col + 2),
        pl.BlockSpec((1, sub, aw),
                     lambda bi, i: (bi, jnp.maximum(i * (tq // sub) - 1, 0), q_col + 1)),
        pl.BlockSpec((1, sub, aw),
                     lambda bi, i: (bi, jnp.minimum((i + 1) * (tq // sub), n_sub - 1), q_col + 1)),
        _resident((3, aw), lambda bi, i: (0, 0)),
        pl.BlockSpec((1, tq, d), lambda bi, i: (bi, i, 0)),
        pl.BlockSpec((1, tq, d), lambda bi, i: (bi, i, 1)),
        _resident((aw, d), lambda bi, i: (0, 0)),
        _resident((aw, d), lambda bi, i: (0, 0)),
        _resident((d, d), lambda bi, i: (0, 0)),
    ]
    band = tq + 2 * WINDOW_BLOCK
    n_tiles = s // tq
    cast_in, cast_out, cast_shapes = _cast_jobs(cast_f32, b * n_tiles, lambda bi, i: bi * n_tiles + i)
    cast_bytes = sum(2 * spec.block_shape[0] * spec.block_shape[1] * (4 + 2) for spec in cast_in)
    est = (2 * (3 * tq * aw + tq * kvw + 2 * WINDOW_BLOCK * kvw + 2 * sub * aw + 3 * tq * d) * 2
           + (2 * aw * d + d * d) * 2
           + (3 * N_KV_HEADS * band * HEAD_DIM + 2 * tq * aw + tq * d) * 2
           + 4 * tq * aw * 4 + cast_bytes)
    outs = pl.pallas_call(
        functools.partial(_mixer_kernel, tq=tq, seq=s, nc=nc, row_groups=4 if nblk % 4 == 0 else 1,
                          n_cast=len(cast_f32)),
        out_shape=[jax.ShapeDtypeStruct((b, s, d), BF16)] + cast_shapes,
        grid=(b, n_tiles),
        in_specs=in_specs + cast_in,
        out_specs=[pl.BlockSpec((1, tq, d), lambda bi, i: (bi, i, 0))] + cast_out,
        scratch_shapes=[
            pltpu.VMEM((N_KV_HEADS, band, HEAD_DIM), BF16),
            pltpu.VMEM((N_KV_HEADS, band, 2 * HEAD_DIM), BF16),
            pltpu.VMEM((tq, aw), BF16),
            pltpu.VMEM((tq, aw), BF16),
            pltpu.VMEM((tq, d), BF16),
        ],
        compiler_params=pltpu.CompilerParams(
            dimension_semantics=("arbitrary", "arbitrary"),
            vmem_limit_bytes=_vmem_limit(est)),
        name="token_mixer",
    )(sink, z3, z3, z3, z3, z3, z3, z3, z3, conv_w, z3, z3, wa, wc, wo, *cast_f32)
    return outs[0], outs[1:]


def _cross_kernel(x_ref, dl_ref, g_ref, wq_ref, kv_ref, wo_ref, o_ref, a_ref, *, nc):
    hd = MEM_HEAD_DIM
    mw = MEM_HEADS * hd
    n_mem = kv_ref.shape[1]
    d = x_ref.shape[1]
    x1 = x_ref[...] + dl_ref[...].astype(F32)
    h = _rms_norm_f32(x1, g_ref[...]).astype(BF16)
    q = jnp.dot(h, wq_ref[...], preferred_element_type=F32) * (MEM_HEAD_DIM ** -0.5 * LOG2_E)
    q = q.astype(BF16)
    ones = jnp.ones((n_mem, hd), BF16)
    for hh in range(MEM_HEADS):
        k = kv_ref[0, :, hh * hd:(hh + 1) * hd]
        v_aug = jnp.concatenate([kv_ref[0, :, mw + hh * hd:mw + (hh + 1) * hd], ones], axis=1)
        s = lax.dot_general(q[:, hh * hd:(hh + 1) * hd], k, (((1,), (1,)), ((), ())),
                            preferred_element_type=F32)
        e = jnp.exp2(s - jnp.max(s, axis=-1, keepdims=True))
        o_aug = jnp.dot(e.astype(BF16), v_aug, preferred_element_type=F32)
        a_ref[:, hh * hd:(hh + 1) * hd] = (o_aug[:, 0:hd] / o_aug[:, hd:]).astype(BF16)
    for c0 in range(0, d, nc):
        cs = slice(c0, c0 + nc)
        o_ref[:, cs] = (x_ref[:, cs] + dl_ref[:, cs].astype(F32)
                        + jnp.dot(a_ref[...], wo_ref[:, cs], preferred_element_type=F32))


def _cross(x2d, delta2d, g, wq, kv3, wo, *, tm, seq, nc):
    m, d = x2d.shape
    n_mem, kvw = kv3.shape[1], kv3.shape[2]
    mw = wq.shape[1]
    tiles_per_batch = seq // tm
    est = (2 * tm * d * (4 + 2 + 4) + 2 * n_mem * kvw * 2 + 2 * d * mw * 2 + tm * mw * 2
           + tm * d * (4 + 2) + 2 * tm * n_mem * 4 + tm * nc * 4)
    return pl.pallas_call(
        functools.partial(_cross_kernel, nc=nc),
        out_shape=jax.ShapeDtypeStruct((m, d), F32),
        grid=(m // tm,),
        in_specs=[
            pl.BlockSpec((tm, d), lambda i: (i, 0)),
            pl.BlockSpec((tm, d), lambda i: (i, 0)),
            _resident((1, d), lambda i: (0, 0)),
            _resident((d, mw), lambda i: (0, 0)),
            pl.BlockSpec((1, n_mem, kvw), lambda i: (i // tiles_per_batch, 0, 0)),
            _resident((mw, d), lambda i: (0, 0)),
        ],
        out_specs=pl.BlockSpec((tm, d), lambda i: (i, 0)),
        scratch_shapes=[pltpu.VMEM((tm, mw), BF16)],
        compiler_params=pltpu.CompilerParams(
            dimension_semantics=("arbitrary",),
            vmem_limit_bytes=_vmem_limit(est)),
        name="cross_attn",
    )(x2d, delta2d, g.reshape(1, d), wq, kv3, wo)


def _ffn_part_kernel(*refs, chunks, first, final_norm):
    if first:
        y_ref, g_ref, wg_ref, wu_ref, wd_ref, gfin_ref, o_ref, h_ref, a_ref = refs
        h_ref[...] = _rms_norm_f32(y_ref[...], g_ref[...]).astype(BF16)
    else:
        y_ref, h_ref, wg_ref, wu_ref, wd_ref, gfin_ref, o_ref, a_ref = refs
    for c0, cw in chunks:
        cs = slice(c0, c0 + cw)
        gate = jnp.dot(h_ref[...], wg_ref[:, cs], preferred_element_type=F32)
        up = jnp.dot(h_ref[...], wu_ref[:, cs], preferred_element_type=F32)
        a_ref[:, cs] = (jax.nn.silu(gate) * up).astype(BF16)
    y = y_ref[...] + jnp.dot(a_ref[...], wd_ref[...], preferred_element_type=F32)
    if final_norm:
        y = _rms_norm_f32(y, gfin_ref[...])
    o_ref[...] = y


def _ffn_part(y2d, h2d, g, wg, wu, wd, g_final, *, part, n_parts, tm, tc, final_norm):
    m, d = y2d.shape
    fp = wg.shape[1] // n_parts
    first = part == 0
    chunks = tuple((c0, min(tc, fp - c0)) for c0 in range(0, fp, tc))
    row = pl.BlockSpec((tm, d), lambda i: (i, 0))
    vec = _resident((1, d), lambda i: (0, 0))
    weights = [
        _resident((d, fp), lambda i: (0, part)),
        _resident((d, fp), lambda i: (0, part)),
        _resident((fp, d), lambda i: (part, 0)),
    ]
    est = 3 * d * fp * 2 + 2 * tm * d * (4 + 4 + 2) + tm * fp * 2 + 2 * tm * tc * 4 + 2 * tm * d * 4
    if first:
        in_specs = [row, vec] + weights + [vec]
        args = (y2d, g.reshape(1, d), wg, wu, wd, g_final.reshape(1, d))
        out_shape = [jax.ShapeDtypeStruct((m, d), F32), jax.ShapeDtypeStruct((m, d), BF16)]
        out_specs = [row, row]
    else:
        in_specs = [row, row] + weights + [vec]
        args = (y2d, h2d, wg, wu, wd, g_final.reshape(1, d))
        out_shape = jax.ShapeDtypeStruct((m, d), F32)
        out_specs = row
    return pl.pallas_call(
        functools.partial(_ffn_part_kernel, chunks=chunks, first=first, final_norm=final_norm),
        out_shape=out_shape,
        grid=(m // tm,),
        in_specs=in_specs,
        out_specs=out_specs,
        scratch_shapes=[pltpu.VMEM((tm, fp), BF16)],
        compiler_params=pltpu.CompilerParams(
            dimension_semantics=("arbitrary",),
            vmem_limit_bytes=_vmem_limit(est)),
        name="ffn_part%d" % part,
    )(*args)


def _tile_plan(seq):
    return dict(
        in_proj_rows=256,
        mem_rows=256,
        mixer_rows=min(512, seq),
        cross_rows=min(1024, seq),
        ffn_rows=512,
        ffn_parts=2,
        chunk=512,
    )


def _rope_tables(s):
    inv = 1.0 / (ROPE_THETA ** (jnp.arange(0, HEAD_DIM, 2, dtype=F32) / HEAD_DIM))
    ang = jnp.arange(s, dtype=F32)[:, None] * inv[None, :]
    cos, sin = jnp.cos(ang), jnp.sin(ang)
    return jnp.concatenate([cos, cos], axis=-1), jnp.concatenate([-sin, sin], axis=-1)


def kernel(x, mem, g_mix, w_in, sink, conv_w, b_gate, w_attn_out, w_conv_out, w_o,
           g_cross, g_mem, w_cq, w_ckv, w_co, g_ffn, w_gate, w_up, w_down, g_final):
    b, s, d = x.shape
    n_mem = mem.shape[1]
    depth = g_mix.shape[0]
    cos_full, sin_signed = _rope_tables(s)
    x2d = x.reshape(b * s, d)
    mem2d = mem.reshape(b * n_mem, d)
    plan = _tile_plan(s)
    chunk, n_parts = plan["chunk"], plan["ffn_parts"]
    for l in range(depth):
        z, (wa, wc, wo, wq, wco, wckv) = _in_proj(
            x2d, g_mix[l], w_in[l].astype(BF16), b_gate[l], cos_full, sin_signed,
            (w_attn_out[l], w_conv_out[l], w_o[l], w_cq[l], w_co[l], w_ckv[l]),
            tm=plan["in_proj_rows"], tn=chunk, seq=s)
        mem_kv = _norm_matmul(mem2d, g_mem[l], wckv, tm=plan["mem_rows"], tn=chunk)
        delta, (wg, wu, wd) = _mixer(z.reshape(b, s, -1), sink[l], conv_w[l], wa, wc, wo,
                                     (w_gate[l], w_up[l], w_down[l]), tq=plan["mixer_rows"], nc=chunk)
        x2d = _cross(x2d, delta.reshape(b * s, d), g_cross[l], wq, mem_kv.reshape(b, n_mem, -1), wco,
                     tm=plan["cross_rows"], seq=s, nc=chunk)
        h2d = None
        for part in range(n_parts):
            out = _ffn_part(x2d, h2d, g_ffn[l], wg, wu, wd, g_final, part=part, n_parts=n_parts,
                            tm=plan["ffn_rows"], tc=chunk,
                            final_norm=(l == depth - 1 and part == n_parts - 1))
            x2d, h2d = out if part == 0 else (out, h2d)
    return x2d.reshape(b, s, d)
```

```python
import functools
import math

import jax
import jax.numpy as jnp
from jax import lax
from jax.experimental import pallas as pl
from jax.experimental.pallas import tpu as pltpu

HEAD_DIM = 128
N_Q_HEADS = 8
N_KV_HEADS = 2
Q_GROUP = N_Q_HEADS // N_KV_HEADS
ATTN_WIDTH = N_Q_HEADS * HEAD_DIM
KV_WIDTH = N_KV_HEADS * HEAD_DIM
WINDOW_BLOCK = 128
ROPE_THETA = 10000.0
MEM_HEADS = 4
MEM_HEAD_DIM = 128
RMS_EPS = 1e-6
NEG_INF = -1e30
LOG2_E = math.log2(math.e)

V7X_VMEM_BYTES = 64 * 1024 * 1024
VMEM_RESERVE_BYTES = 2 * 1024 * 1024
VMEM_TEMPORARIES_BYTES = 12 * 1024 * 1024
BF16_SUBLANE_TILE = 16
F32_SUBLANE_TILE = 8

F32 = jnp.float32
BF16 = jnp.bfloat16


def _vmem_limit(estimate_bytes):
    return int(min(estimate_bytes + VMEM_TEMPORARIES_BYTES, V7X_VMEM_BYTES - VMEM_RESERVE_BYTES))


def _resident(block_shape, index_map):
    return pl.BlockSpec(block_shape, index_map, pipeline_mode=pl.Buffered(1))


def _rms_norm_f32(x, g):
    y = x * lax.rsqrt(jnp.mean(x * x, axis=-1, keepdims=True) + RMS_EPS)
    return y * g


def _rope(t, cos_full, sin_signed):
    return t * cos_full + pltpu.roll(t, HEAD_DIM // 2, axis=1) * sin_signed


def _cast_jobs(arrays, n_steps, step_index):
    in_specs, out_specs, out_shapes = [], [], []
    for a in arrays:
        r, c = a.shape
        rb = BF16_SUBLANE_TILE
        while r % rb or r // rb > n_steps:
            rb += BF16_SUBLANE_TILE
        n_chunks = r // rb

        def imap(*idx, n_chunks=n_chunks):
            return (jnp.minimum(step_index(*idx), n_chunks - 1), 0)

        in_specs.append(pl.BlockSpec((rb, c), imap))
        out_specs.append(pl.BlockSpec((rb, c), imap))
        out_shapes.append(jax.ShapeDtypeStruct((r, c), BF16))
    return in_specs, out_specs, out_shapes


def _run_cast_jobs(src_refs, dst_refs):
    for src, dst in zip(src_refs, dst_refs):
        dst[...] = src[...].astype(BF16)


def _in_proj_kernel(*refs, tn, d, n_cast):
    x_ref, g_ref, w_ref, bg_ref, cos_ref, sin_ref = refs[:6]
    cast_src = refs[6:6 + n_cast]
    o_ref = refs[6 + n_cast]
    cast_dst = refs[7 + n_cast:7 + 2 * n_cast]
    h_ref = refs[7 + 2 * n_cast]
    _run_cast_jobs(cast_src, cast_dst)

    aw, kw, hd = ATTN_WIDTH, KV_WIDTH, HEAD_DIM
    src_q, src_k, src_cu = 0, aw, aw + 2 * kw
    src_cb, src_cc, src_gl = src_cu + aw, src_cu + 2 * aw, src_cu + 3 * aw
    dst_q, dst_u, dst_cb, dst_kv = 2 * d, 2 * d + aw, 2 * d + 2 * aw, 2 * d + 3 * aw

    x = x_ref[...]
    h_ref[...] = (x * g_ref[...]).astype(BF16)
    inv_rms = lax.rsqrt(jnp.mean(x * x, axis=-1, keepdims=True) + RMS_EPS)
    inv_rms = jnp.broadcast_to(inv_rms, (x.shape[0], HEAD_DIM))
    inv_rms_wide = jnp.concatenate([inv_rms] * (tn // HEAD_DIM), axis=1)

    def proj(col0, width):
        return jnp.dot(h_ref[...], w_ref[:, col0:col0 + width], preferred_element_type=F32)

    cos = cos_ref[...]
    sin = sin_ref[...]
    q_scale = HEAD_DIM ** -0.5 * LOG2_E

    for c0 in range(0, 2 * d, tn):
        o_ref[:, c0:c0 + tn] = jax.nn.sigmoid(
            proj(src_gl + c0, tn) * inv_rms_wide + bg_ref[:, c0:c0 + tn]).astype(BF16)
    acc = proj(src_k, 2 * kw)
    parts = ([_rope(acc[:, k0:k0 + hd], cos, sin) * inv_rms for k0 in range(0, kw, hd)]
             + [acc[:, k0:k0 + hd] * inv_rms for k0 in range(kw, 2 * kw, hd)])
    o_ref[:, dst_kv:] = jnp.concatenate(parts, axis=1).astype(BF16)
    for c0 in range(0, aw, tn):
        acc = proj(src_q + c0, tn)
        heads = [_rope(acc[:, k0:k0 + hd], cos, sin) * (inv_rms * q_scale) for k0 in range(0, tn, hd)]
        o_ref[:, dst_q + c0:dst_q + c0 + tn] = jnp.concatenate(heads, axis=1).astype(BF16)
    for c0 in range(0, aw, tn):
        o_ref[:, dst_u + c0:dst_u + c0 + tn] = (
            (proj(src_cc + c0, tn) * inv_rms_wide) * (proj(src_cu + c0, tn) * inv_rms_wide)).astype(BF16)
    for c0 in range(0, aw, tn):
        o_ref[:, dst_cb + c0:dst_cb + c0 + tn] = (proj(src_cb + c0, tn) * inv_rms_wide).astype(BF16)


def _in_proj(x2d, g, w_bf16, b_gate, cos_full, sin_signed, cast_f32, *, tm, tn, seq):
    m, d = x2d.shape
    n_in = w_bf16.shape[1]
    n_out = n_in - ATTN_WIDTH
    tiles_per_seq = seq // tm
    cast_in, cast_out, cast_shapes = _cast_jobs(cast_f32, m // tm, lambda i: i)
    cast_bytes = sum(2 * spec.block_shape[0] * spec.block_shape[1] * (4 + 2) for spec in cast_in)
    est = (2 * tm * d * 4 + tm * d * 2 + d * n_in * 2 + 2 * tm * n_out * 2 + 4 * tm * HEAD_DIM * 4
           + 4 * tm * tn * 4 + cast_bytes)
    outs = pl.pallas_call(
        functools.partial(_in_proj_kernel, tn=tn, d=d, n_cast=len(cast_f32)),
        out_shape=[jax.ShapeDtypeStruct((m, n_out), BF16)] + cast_shapes,
        grid=(m // tm,),
        in_specs=[
            pl.BlockSpec((tm, d), lambda i: (i, 0)),
            _resident((1, d), lambda i: (0, 0)),
            _resident((d, n_in), lambda i: (0, 0)),
            _resident((1, 2 * d), lambda i: (0, 0)),
            pl.BlockSpec((tm, HEAD_DIM), lambda i: (i % tiles_per_seq, 0)),
            pl.BlockSpec((tm, HEAD_DIM), lambda i: (i % tiles_per_seq, 0)),
        ] + cast_in,
        out_specs=[pl.BlockSpec((tm, n_out), lambda i: (i, 0))] + cast_out,
        scratch_shapes=[pltpu.VMEM((tm, d), BF16)],
        compiler_params=pltpu.CompilerParams(
            dimension_semantics=("arbitrary",),
            vmem_limit_bytes=_vmem_limit(est)),
        name="in_proj",
    )(x2d, g.reshape(1, d), w_bf16, b_gate.reshape(1, 2 * d), cos_full, sin_signed, *cast_f32)
    return outs[0], outs[1:]


def _norm_matmul_kernel(x_ref, g_ref, w_ref, o_ref, *, tn):
    x = x_ref[...]
    h = (x * g_ref[...]).astype(BF16)
    inv_rms = lax.rsqrt(jnp.mean(x * x, axis=-1, keepdims=True) + RMS_EPS)
    inv_rms = jnp.concatenate([jnp.broadcast_to(inv_rms, (x.shape[0], HEAD_DIM))] * (tn // HEAD_DIM), axis=1)
    for c0 in range(0, o_ref.shape[1], tn):
        o_ref[:, c0:c0 + tn] = (jnp.dot(h, w_ref[:, c0:c0 + tn], preferred_element_type=F32)
                                * inv_rms).astype(BF16)


def _norm_matmul(x2d, g, w_bf16, *, tm, tn):
    m, d = x2d.shape
    n = w_bf16.shape[1]
    est = 2 * tm * d * 4 + tm * d * 2 + d * n * 2 + 2 * tm * n * 2 + 2 * tm * tn * 4
    return pl.pallas_call(
        functools.partial(_norm_matmul_kernel, tn=tn),
        out_shape=jax.ShapeDtypeStruct((m, n), BF16),
        grid=(m // tm,),
        in_specs=[
            pl.BlockSpec((tm, d), lambda i: (i, 0)),
            _resident((1, d), lambda i: (0, 0)),
            _resident((d, n), lambda i: (0, 0)),
        ],
        out_specs=pl.BlockSpec((tm, n), lambda i: (i, 0)),
        compiler_params=pltpu.CompilerParams(
            dimension_semantics=("arbitrary",),
            vmem_limit_bytes=_vmem_limit(est)),
        name="norm_matmul",
    )(x2d, g.reshape(1, d), w_bf16)


def _mixer_kernel(*refs, tq, seq, nc, row_groups, n_cast):
    (sink_ref, q_ref, kvc_ref, kvp_ref, kvn_ref, u_ref, cb_ref, up_ref, un_ref,
     cw_ref, ga_ref, gc_ref, wa_ref, wc_ref, wo_ref) = refs[:15]
    cast_src = refs[15:15 + n_cast]
    o_ref = refs[15 + n_cast]
    cast_dst = refs[16 + n_cast:16 + 2 * n_cast]
    kb_ref, vb_ref, attn_ref, conv_ref, m_ref = refs[16 + 2 * n_cast:]
    _run_cast_jobs(cast_src, cast_dst)

    i = pl.program_id(1)
    n_tiles = seq // tq
    nblk = tq // WINDOW_BLOCK
    blk = WINDOW_BLOCK
    hd = HEAD_DIM
    sub = F32_SUBLANE_TILE
    has_prev = i > 0
    has_next = i < n_tiles - 1

    u = u_ref[0].astype(F32)
    last = BF16_SUBLANE_TILE - 1
    u_prev = jnp.where(has_prev, up_ref[0, last:last + 1, :].astype(F32), 0.0)
    u_next = jnp.where(has_next, un_ref[0, 0:1, :].astype(F32), 0.0)
    rolled_dn = pltpu.roll(u, 1, axis=0)
    rolled_up = pltpu.roll(u, tq - 1, axis=0)
    row = lax.broadcasted_iota(jnp.int32, (sub, u.shape[1]), 0)
    u_dn = jnp.concatenate([jnp.where(row == 0, u_prev, rolled_dn[0:sub]), rolled_dn[sub:]], axis=0)
    u_up = jnp.concatenate([rolled_up[:tq - sub], jnp.where(row == sub - 1, u_next, rolled_up[tq - sub:])],
                           axis=0)
    conv = u_dn * cw_ref[0:1, :] + u * cw_ref[1:2, :] + u_up * cw_ref[2:3, :]
    conv_ref[...] = (cb_ref[0].astype(F32) * conv).astype(BF16)

    for h in range(N_KV_HEADS):
        ks = slice(h * hd, (h + 1) * hd)
        vs = slice((N_KV_HEADS + h) * hd, (N_KV_HEADS + h + 1) * hd)
        kb_ref[h, 0:blk, :] = kvp_ref[0, :, ks]
        kb_ref[h, blk:blk + tq, :] = kvc_ref[0, :, ks]
        kb_ref[h, blk + tq:, :] = kvn_ref[0, :, ks]
        vb_ref[h, 0:blk, 0:hd] = kvp_ref[0, :, vs]
        vb_ref[h, blk:blk + tq, 0:hd] = kvc_ref[0, :, vs]
        vb_ref[h, blk + tq:, 0:hd] = kvn_ref[0, :, vs]
        vb_ref[h, :, hd:] = jnp.ones((tq + 2 * blk, hd), BF16)

    rows = Q_GROUP * blk
    qp = lax.broadcasted_iota(jnp.int32, (rows, blk), 0) % blk
    kp = lax.broadcasted_iota(jnp.int32, (rows, blk), 1)
    bias_prev = jnp.where(kp >= qp, 0.0, NEG_INF)
    bias_next = jnp.where(kp <= qp, 0.0, NEG_INF)
    bias_prev_edge = jnp.where(has_prev, bias_prev, NEG_INF)
    bias_next_edge = jnp.where(has_next, bias_next, NEG_INF)

    def attend(h, j, sink_b):
        q_stack = jnp.concatenate(
            [q_ref[0, j * blk:(j + 1) * blk, (h * Q_GROUP + g) * hd:(h * Q_GROUP + g + 1) * hd]
             for g in range(Q_GROUP)], axis=0)
        s = lax.dot_general(q_stack, kb_ref[h, j * blk:(j + 3) * blk, :],
                            (((1,), (1,)), ((), ())), preferred_element_type=F32)
        s_prev = s[:, 0:blk] + (bias_prev_edge if j == 0 else bias_prev)
        s_cur = s[:, blk:2 * blk]
        s_next = s[:, 2 * blk:] + (bias_next_edge if j == nblk - 1 else bias_next)
        m = jnp.max(jnp.maximum(jnp.maximum(s_prev, s_cur), s_next), axis=-1, keepdims=True)
        m = jnp.maximum(m, sink_b)
        e = jnp.concatenate(
            [jnp.exp2(s_prev - m), jnp.exp2(s_cur - m), jnp.exp2(s_next - m)], axis=1)
        o_aug = jnp.dot(e.astype(BF16), vb_ref[h, j * blk:(j + 3) * blk, :],
                        preferred_element_type=F32)
        o = o_aug[:, 0:hd] / (o_aug[:, hd:] + jnp.exp2(sink_b - m))
        for g in range(Q_GROUP):
            qh = h * Q_GROUP + g
            attn_ref[j * blk:(j + 1) * blk, qh * hd:(qh + 1) * hd] = (
                o[g * blk:(g + 1) * blk, :].astype(BF16))

    def projection_steps(rs):
        d = wo_ref.shape[1]

        def merge(cs):
            yc = jnp.dot(conv_ref[rs, :], wc_ref[:, cs], preferred_element_type=F32)
            ya = jnp.dot(attn_ref[rs, :], wa_ref[:, cs], preferred_element_type=F32)
            m_ref[rs, cs] = (ga_ref[0, rs, cs].astype(F32) * ya
                             + gc_ref[0, rs, cs].astype(F32) * yc).astype(BF16)

        def out(cs):
            o_ref[0, rs, cs] = jnp.dot(m_ref[rs, :], wo_ref[:, cs],
                                       preferred_element_type=F32).astype(BF16)

        chunks = [slice(c0, c0 + nc) for c0 in range(0, d, nc)]
        return ([functools.partial(merge, cs) for cs in chunks]
                + [functools.partial(out, cs) for cs in chunks])

    sink_bs = [jnp.concatenate(
        [jnp.full((blk, hd), sink_ref[h * Q_GROUP + g] * LOG2_E, F32) for g in range(Q_GROUP)], axis=0)
        for h in range(N_KV_HEADS)]
    blocks_per_group = nblk // row_groups

    def attention_steps(r):
        return [functools.partial(attend, h, j, sink_bs[h])
                for j in range(r * blocks_per_group, (r + 1) * blocks_per_group)
                for h in range(N_KV_HEADS)]

    for step in attention_steps(0):
        step()
    for r in range(row_groups):
        proj = projection_steps(slice(r * blocks_per_group * blk, (r + 1) * blocks_per_group * blk))
        att = attention_steps(r + 1) if r + 1 < row_groups else []
        for k in range(max(len(proj), len(att))):
            if k < len(proj):
                proj[k]()
            if k < len(att):
                att[k]()


def _mixer(z3, sink, conv_w, wa, wc, wo, cast_f32, *, tq, nc):
    b, s, _ = z3.shape
    d = wo.shape[1]
    aw = ATTN_WIDTH
    kvw = 2 * KV_WIDTH
    q_col = 2 * d // aw
    kv_col = (2 * d + 3 * aw) // kvw
    nblk = tq // WINDOW_BLOCK
    n_win = s // WINDOW_BLOCK
    sub = BF16_SUBLANE_TILE
    n_sub = s // sub

    def tile(col):
        return pl.BlockSpec((1, tq, aw), lambda bi, i, col=col: (bi, i, col))

    in_specs = [
        pl.BlockSpec(memory_space=pltpu.SMEM),
        tile(q_col),
        pl.BlockSpec((1, tq, kvw), lambda bi, i: (bi, i, kv_col)),
        pl.BlockSpec((1, WINDOW_BLOCK, kvw),
                     lambda bi, i: (bi, jnp.maximum(i * nblk - 1, 0), kv_col)),
        pl.BlockSpec((1, WINDOW_BLOCK, kvw),
                     lambda bi, i: (bi, jnp.minimum((i + 1) * nblk, n_win - 1), kv_col)),
        tile(q_col + 1), tile(q_col + 2),
        pl.BlockSpec((1, sub, aw),
                     lambda bi, i: (bi, jnp.maximum(i * (tq // sub) - 1, 0), q_col + 1)),
        pl.BlockSpec((1, sub, aw),
                     lambda bi, i: (bi, jnp.minimum((i + 1) * (tq // sub), n_sub - 1), q_col + 1)),
        _resident((3, aw), lambda bi, i: (0, 0)),
        pl.BlockSpec((1, tq, d), lambda bi, i: (bi, i, 0)),
        pl.BlockSpec((1, tq, d), lambda bi, i: (bi, i, 1)),
        _resident((aw, d), lambda bi, i: (0, 0)),
        _resident((aw, d), lambda bi, i: (0, 0)),
        _resident((d, d), lambda bi, i: (0, 0)),
    ]
    band = tq + 2 * WINDOW_BLOCK
    n_tiles = s // tq
    cast_in, cast_out, cast_shapes = _cast_jobs(cast_f32, b * n_tiles, lambda bi, i: bi * n_tiles + i)
    cast_bytes = sum(2 * spec.block_shape[0] * spec.block_shape[1] * (4 + 2) for spec in cast_in)
    est = (2 * (3 * tq * aw + tq * kvw + 2 * WINDOW_BLOCK * kvw + 2 * sub * aw + 3 * tq * d) * 2
           + (2 * aw * d + d * d) * 2
           + (3 * N_KV_HEADS * band * HEAD_DIM + 2 * tq * aw + tq * d) * 2
           + 4 * tq * aw * 4 + cast_bytes)
    outs = pl.pallas_call(
        functools.partial(_mixer_kernel, tq=tq, seq=s, nc=nc, row_groups=4 if nblk % 4 == 0 else 1,
                          n_cast=len(cast_f32)),
        out_shape=[jax.ShapeDtypeStruct((b, s, d), BF16)] + cast_shapes,
        grid=(b, n_tiles),
        in_specs=in_specs + cast_in,
        out_specs=[pl.BlockSpec((1, tq, d), lambda bi, i: (bi, i, 0))] + cast_out,
        scratch_shapes=[
            pltpu.VMEM((N_KV_HEADS, band, HEAD_DIM), BF16),
            pltpu.VMEM((N_KV_HEADS, band, 2 * HEAD_DIM), BF16),
            pltpu.VMEM((tq, aw), BF16),
            pltpu.VMEM((tq, aw), BF16),
            pltpu.VMEM((tq, d), BF16),
        ],
        compiler_params=pltpu.CompilerParams(
            dimension_semantics=("arbitrary", "arbitrary"),
            vmem_limit_bytes=_vmem_limit(est)),
        name="token_mixer",
    )(sink, z3, z3, z3, z3, z3, z3, z3, z3, conv_w, z3, z3, wa, wc, wo, *cast_f32)
    return outs[0], outs[1:]


def _cross_kernel(x_ref, dl_ref, g_ref, wq_ref, kv_ref, wo_ref, o_ref, a_ref, *, nc):
    hd = MEM_HEAD_DIM
    mw = MEM_HEADS * hd
    n_mem = kv_ref.shape[1]
    d = x_ref.shape[1]
    x1 = x_ref[...] + dl_ref[...].astype(F32)
    h = (x1 * g_ref[...]).astype(BF16)
    inv_rms = lax.rsqrt(jnp.mean(x1 * x1, axis=-1, keepdims=True) + RMS_EPS)
    inv_rms = jnp.concatenate([jnp.broadcast_to(inv_rms, (x1.shape[0], hd))] * MEM_HEADS, axis=1)
    q = jnp.dot(h, wq_ref[...], preferred_element_type=F32) * (inv_rms * (MEM_HEAD_DIM ** -0.5 * LOG2_E))
    q = q.astype(BF16)
    ones = jnp.ones((n_mem, hd), BF16)
    for hh in range(MEM_HEADS):
        k = kv_ref[0, :, hh * hd:(hh + 1) * hd]
        v_aug = jnp.concatenate([kv_ref[0, :, mw + hh * hd:mw + (hh + 1) * hd], ones], axis=1)
        s = lax.dot_general(q[:, hh * hd:(hh + 1) * hd], k, (((1,), (1,)), ((), ())),
                            preferred_element_type=F32)
        e = jnp.exp2(s - jnp.max(s, axis=-1, keepdims=True))
        o_aug = jnp.dot(e.astype(BF16), v_aug, preferred_element_type=F32)
        a_ref[:, hh * hd:(hh + 1) * hd] = (o_aug[:, 0:hd] / o_aug[:, hd:]).astype(BF16)
    for c0 in range(0, d, nc):
        cs = slice(c0, c0 + nc)
        o_ref[:, cs] = (x_ref[:, cs] + dl_ref[:, cs].astype(F32)
                        + jnp.dot(a_ref[...], wo_ref[:, cs], preferred_element_type=F32))


def _cross(x2d, delta2d, g, wq, kv3, wo, *, tm, seq, nc):
    m, d = x2d.shape
    n_mem, kvw = kv3.shape[1], kv3.shape[2]
    mw = wq.shape[1]
    tiles_per_batch = seq // tm
    est = (2 * tm * d * (4 + 2 + 4) + 2 * n_mem * kvw * 2 + 2 * d * mw * 2 + tm * mw * 2
           + tm * d * (4 + 2) + 2 * tm * n_mem * 4 + tm * nc * 4)
    return pl.pallas_call(
        functools.partial(_cross_kernel, nc=nc),
        out_shape=jax.ShapeDtypeStruct((m, d), F32),
        grid=(m // tm,),
        in_specs=[
            pl.BlockSpec((tm, d), lambda i: (i, 0)),
            pl.BlockSpec((tm, d), lambda i: (i, 0)),
            _resident((1, d), lambda i: (0, 0)),
            _resident((d, mw), lambda i: (0, 0)),
            pl.BlockSpec((1, n_mem, kvw), lambda i: (i // tiles_per_batch, 0, 0)),
            _resident((mw, d), lambda i: (0, 0)),
        ],
        out_specs=pl.BlockSpec((tm, d), lambda i: (i, 0)),
        scratch_shapes=[pltpu.VMEM((tm, mw), BF16)],
        compiler_params=pltpu.CompilerParams(
            dimension_semantics=("arbitrary",),
            vmem_limit_bytes=_vmem_limit(est)),
        name="cross_attn",
    )(x2d, delta2d, g.reshape(1, d), wq, kv3, wo)


def _ffn_part_kernel(*refs, chunks, first, final_norm):
    if first:
        y_ref, g_ref, wg_ref, wu_ref, wd_ref, gfin_ref, o_ref, h_ref, r_ref, a_ref = refs
        y = y_ref[...]
        h_ref[...] = (y * g_ref[...]).astype(BF16)
        inv_rms = lax.rsqrt(jnp.mean(y * y, axis=-1, keepdims=True) + RMS_EPS)
        r_ref[...] = jnp.broadcast_to(inv_rms, r_ref.shape)
    else:
        y_ref, h_ref, r_ref, wg_ref, wu_ref, wd_ref, gfin_ref, o_ref, a_ref = refs
    lanes = r_ref.shape[1]
    for c0, cw in chunks:
        cs = slice(c0, c0 + cw)
        inv_rms = jnp.concatenate([r_ref[...]] * (cw // lanes), axis=1)
        gate = jnp.dot(h_ref[...], wg_ref[:, cs], preferred_element_type=F32) * inv_rms
        up = jnp.dot(h_ref[...], wu_ref[:, cs], preferred_element_type=F32) * inv_rms
        a_ref[:, cs] = (jax.nn.silu(gate) * up).astype(BF16)
    y = y_ref[...] + jnp.dot(a_ref[...], wd_ref[...], preferred_element_type=F32)
    if final_norm:
        y = _rms_norm_f32(y, gfin_ref[...])
    o_ref[...] = y


def _ffn_part(y2d, hr, g, wg, wu, wd, g_final, *, part, n_parts, tm, tc, final_norm):
    m, d = y2d.shape
    fp = wg.shape[1] // n_parts
    first = part == 0
    chunks = tuple((c0, min(tc, fp - c0)) for c0 in range(0, fp, tc))
    row = pl.BlockSpec((tm, d), lambda i: (i, 0))
    stat = pl.BlockSpec((tm, HEAD_DIM), lambda i: (i, 0))
    vec = _resident((1, d), lambda i: (0, 0))
    weights = [
        _resident((d, fp), lambda i: (0, part)),
        _resident((d, fp), lambda i: (0, part)),
        _resident((fp, d), lambda i: (part, 0)),
    ]
    est = (3 * d * fp * 2 + 2 * tm * d * (4 + 4 + 2) + 2 * tm * HEAD_DIM * 4 + tm * fp * 2
           + 2 * tm * tc * 4 + 2 * tm * d * 4)
    if first:
        in_specs = [row, vec] + weights + [vec]
        args = (y2d, g.reshape(1, d), wg, wu, wd, g_final.reshape(1, d))
        out_shape = [jax.ShapeDtypeStruct((m, d), F32), jax.ShapeDtypeStruct((m, d), BF16),
                     jax.ShapeDtypeStruct((m, HEAD_DIM), F32)]
        out_specs = [row, row, stat]
    else:
        in_specs = [row, row, stat] + weights + [vec]
        args = (y2d, hr[0], hr[1], wg, wu, wd, g_final.reshape(1, d))
        out_shape = jax.ShapeDtypeStruct((m, d), F32)
        out_specs = row
    return pl.pallas_call(
        functools.partial(_ffn_part_kernel, chunks=chunks, first=first, final_norm=final_norm),
        out_shape=out_shape,
        grid=(m // tm,),
        in_specs=in_specs,
        out_specs=out_specs,
        scratch_shapes=[pltpu.VMEM((tm, fp), BF16)],
        compiler_params=pltpu.CompilerParams(
            dimension_semantics=("arbitrary",),
            vmem_limit_bytes=_vmem_limit(est)),
        name="ffn_part%d" % part,
    )(*args)


def _tile_plan(seq):
    return dict(
        in_proj_rows=256,
        mem_rows=256,
        mixer_rows=min(512, seq),
        cross_rows=min(1024, seq),
        ffn_rows=512,
        ffn_parts=2,
        chunk=512,
    )


def _rope_tables(s):
    inv = 1.0 / (ROPE_THETA ** (jnp.arange(0, HEAD_DIM, 2, dtype=F32) / HEAD_DIM))
    ang = jnp.arange(s, dtype=F32)[:, None] * inv[None, :]
    cos, sin = jnp.cos(ang), jnp.sin(ang)
    return jnp.concatenate([cos, cos], axis=-1), jnp.concatenate([-sin, sin], axis=-1)


def kernel(x, mem, g_mix, w_in, sink, conv_w, b_gate, w_attn_out, w_conv_out, w_o,
           g_cross, g_mem, w_cq, w_ckv, w_co, g_ffn, w_gate, w_up, w_down, g_final):
    b, s, d = x.shape
    n_mem = mem.shape[1]
    depth = g_mix.shape[0]
    cos_full, sin_signed = _rope_tables(s)
    x2d = x.reshape(b * s, d)
    mem2d = mem.reshape(b * n_mem, d)
    plan = _tile_plan(s)
    chunk, n_parts = plan["chunk"], plan["ffn_parts"]
    for l in range(depth):
        z, (wa, wc, wo, wq, wco, wckv) = _in_proj(
            x2d, g_mix[l], w_in[l].astype(BF16), b_gate[l], cos_full, sin_signed,
            (w_attn_out[l], w_conv_out[l], w_o[l], w_cq[l], w_co[l], w_ckv[l]),
            tm=plan["in_proj_rows"], tn=chunk, seq=s)
        mem_kv = _norm_matmul(mem2d, g_mem[l], wckv, tm=plan["mem_rows"], tn=chunk)
        delta, (wg, wu, wd) = _mixer(z.reshape(b, s, -1), sink[l], conv_w[l], wa, wc, wo,
                                     (w_gate[l], w_up[l], w_down[l]), tq=plan["mixer_rows"], nc=chunk)
        x2d = _cross(x2d, delta.reshape(b * s, d), g_cross[l], wq, mem_kv.reshape(b, n_mem, -1), wco,
                     tm=plan["cross_rows"], seq=s, nc=chunk)
        hr = None
        for part in range(n_parts):
            out = _ffn_part(x2d, hr, g_ffn[l], wg, wu, wd, g_final, part=part, n_parts=n_parts,
                            tm=plan["ffn_rows"], tc=chunk,
                            final_norm=(l == depth - 1 and part == n_parts - 1))
            x2d, hr = (out[0], out[1:]) if part == 0 else (out, hr)
    return x2d.reshape(b, s, d)
```

```python
import functools
import math

import jax
import jax.numpy as jnp
from jax import lax
from jax.experimental import pallas as pl
from jax.experimental.pallas import tpu as pltpu

HEAD_DIM = 128
N_Q_HEADS = 8
N_KV_HEADS = 2
Q_GROUP = N_Q_HEADS // N_KV_HEADS
ATTN_WIDTH = N_Q_HEADS * HEAD_DIM
KV_WIDTH = N_KV_HEADS * HEAD_DIM
WINDOW_BLOCK = 128
ROPE_THETA = 10000.0
MEM_HEADS = 4
MEM_HEAD_DIM = 128
RMS_EPS = 1e-6
NEG_INF = -1e30
LOG2_E = math.log2(math.e)

V7X_VMEM_BYTES = 64 * 1024 * 1024
VMEM_RESERVE_BYTES = 2 * 1024 * 1024
VMEM_TEMPORARIES_BYTES = 12 * 1024 * 1024
BF16_SUBLANE_TILE = 16
F32_SUBLANE_TILE = 8

F32 = jnp.float32
BF16 = jnp.bfloat16


def _vmem_limit(estimate_bytes):
    return int(min(estimate_bytes + VMEM_TEMPORARIES_BYTES, V7X_VMEM_BYTES - VMEM_RESERVE_BYTES))


def _resident(block_shape, index_map):
    return pl.BlockSpec(block_shape, index_map, pipeline_mode=pl.Buffered(1))


def _rms_norm_f32(x, g):
    y = x * lax.rsqrt(jnp.mean(x * x, axis=-1, keepdims=True) + RMS_EPS)
    return y * g


def _rope(t, cos_full, sin_signed):
    return t * cos_full + pltpu.roll(t, HEAD_DIM // 2, axis=1) * sin_signed


def _cast_jobs(arrays, n_steps, step_index):
    in_specs, out_specs, out_shapes = [], [], []
    for a in arrays:
        r, c = a.shape
        rb = BF16_SUBLANE_TILE
        while r % rb or r // rb > n_steps:
            rb += BF16_SUBLANE_TILE
        n_chunks = r // rb

        def imap(*idx, n_chunks=n_chunks):
            return (jnp.minimum(step_index(*idx), n_chunks - 1), 0)

        in_specs.append(pl.BlockSpec((rb, c), imap))
        out_specs.append(pl.BlockSpec((rb, c), imap))
        out_shapes.append(jax.ShapeDtypeStruct((r, c), BF16))
    return in_specs, out_specs, out_shapes


def _run_cast_jobs(src_refs, dst_refs):
    for src, dst in zip(src_refs, dst_refs):
        dst[...] = src[...].astype(BF16)


def _in_proj_kernel(*refs, tn, d, n_cast):
    x_ref, g_ref, w_ref, bg_ref, cos_ref, sin_ref = refs[:6]
    cast_src = refs[6:6 + n_cast]
    o_ref = refs[6 + n_cast]
    cast_dst = refs[7 + n_cast:7 + 2 * n_cast]
    h_ref = refs[7 + 2 * n_cast]
    _run_cast_jobs(cast_src, cast_dst)

    aw, kw, hd = ATTN_WIDTH, KV_WIDTH, HEAD_DIM
    src_q, src_k, src_cu = 0, aw, aw + 2 * kw
    src_cb, src_cc, src_gl = src_cu + aw, src_cu + 2 * aw, src_cu + 3 * aw
    dst_q, dst_u, dst_cb, dst_kv = 2 * d, 2 * d + aw, 2 * d + 2 * aw, 2 * d + 3 * aw

    x = x_ref[...]
    h_ref[...] = (x * g_ref[...]).astype(BF16)
    inv_rms = lax.rsqrt(jnp.mean(x * x, axis=-1, keepdims=True) + RMS_EPS)
    inv_rms = jnp.broadcast_to(inv_rms, (x.shape[0], HEAD_DIM))
    inv_rms_wide = jnp.concatenate([inv_rms] * (tn // HEAD_DIM), axis=1)

    def proj(col0, width):
        return jnp.dot(h_ref[...], w_ref[:, col0:col0 + width], preferred_element_type=F32)

    cos = cos_ref[...]
    sin = sin_ref[...]
    q_scale = HEAD_DIM ** -0.5 * LOG2_E

    for c0 in range(0, 2 * d, tn):
        o_ref[:, c0:c0 + tn] = jax.nn.sigmoid(
            proj(src_gl + c0, tn) * inv_rms_wide + bg_ref[:, c0:c0 + tn]).astype(BF16)
    acc = proj(src_k, 2 * kw)
    parts = ([_rope(acc[:, k0:k0 + hd], cos, sin) * inv_rms for k0 in range(0, kw, hd)]
             + [acc[:, k0:k0 + hd] * inv_rms for k0 in range(kw, 2 * kw, hd)])
    o_ref[:, dst_kv:] = jnp.concatenate(parts, axis=1).astype(BF16)
    for c0 in range(0, aw, tn):
        acc = proj(src_q + c0, tn)
        heads = [_rope(acc[:, k0:k0 + hd], cos, sin) * (inv_rms * q_scale) for k0 in range(0, tn, hd)]
        o_ref[:, dst_q + c0:dst_q + c0 + tn] = jnp.concatenate(heads, axis=1).astype(BF16)
    for c0 in range(0, aw, tn):
        o_ref[:, dst_u + c0:dst_u + c0 + tn] = (
            (proj(src_cc + c0, tn) * inv_rms_wide) * (proj(src_cu + c0, tn) * inv_rms_wide)).astype(BF16)
    for c0 in range(0, aw, tn):
        o_ref[:, dst_cb + c0:dst_cb + c0 + tn] = (proj(src_cb + c0, tn) * inv_rms_wide).astype(BF16)


def _in_proj(x2d, g, w_bf16, b_gate, cos_full, sin_signed, cast_f32, *, tm, tn, seq):
    m, d = x2d.shape
    n_in = w_bf16.shape[1]
    n_out = n_in - ATTN_WIDTH
    tiles_per_seq = seq // tm
    cast_in, cast_out, cast_shapes = _cast_jobs(cast_f32, m // tm, lambda i: i)
    cast_bytes = sum(2 * spec.block_shape[0] * spec.block_shape[1] * (4 + 2) for spec in cast_in)
    est = (2 * tm * d * 4 + tm * d * 2 + d * n_in * 2 + 2 * tm * n_out * 2 + 4 * tm * HEAD_DIM * 4
           + 4 * tm * tn * 4 + cast_bytes)
    outs = pl.pallas_call(
        functools.partial(_in_proj_kernel, tn=tn, d=d, n_cast=len(cast_f32)),
        out_shape=[jax.ShapeDtypeStruct((m, n_out), BF16)] + cast_shapes,
        grid=(m // tm,),
        in_specs=[
            pl.BlockSpec((tm, d), lambda i: (i, 0)),
            _resident((1, d), lambda i: (0, 0)),
            _resident((d, n_in), lambda i: (0, 0)),
            _resident((1, 2 * d), lambda i: (0, 0)),
            pl.BlockSpec((tm, HEAD_DIM), lambda i: (i % tiles_per_seq, 0)),
            pl.BlockSpec((tm, HEAD_DIM), lambda i: (i % tiles_per_seq, 0)),
        ] + cast_in,
        out_specs=[pl.BlockSpec((tm, n_out), lambda i: (i, 0))] + cast_out,
        scratch_shapes=[pltpu.VMEM((tm, d), BF16)],
        compiler_params=pltpu.CompilerParams(
            dimension_semantics=("arbitrary",),
            vmem_limit_bytes=_vmem_limit(est)),
        name="in_proj",
    )(x2d, g.reshape(1, d), w_bf16, b_gate.reshape(1, 2 * d), cos_full, sin_signed, *cast_f32)
    return outs[0], outs[1:]


def _norm_matmul_kernel(x_ref, g_ref, w_ref, o_ref, *, tn):
    x = x_ref[...]
    h = (x * g_ref[...]).astype(BF16)
    inv_rms = lax.rsqrt(jnp.mean(x * x, axis=-1, keepdims=True) + RMS_EPS)
    inv_rms = jnp.concatenate([jnp.broadcast_to(inv_rms, (x.shape[0], HEAD_DIM))] * (tn // HEAD_DIM), axis=1)
    for c0 in range(0, o_ref.shape[1], tn):
        o_ref[:, c0:c0 + tn] = (jnp.dot(h, w_ref[:, c0:c0 + tn], preferred_element_type=F32)
                                * inv_rms).astype(BF16)


def _norm_matmul(x2d, g, w_bf16, *, tm, tn):
    m, d = x2d.shape
    n = w_bf16.shape[1]
    est = 2 * tm * d * 4 + tm * d * 2 + d * n * 2 + 2 * tm * n * 2 + 2 * tm * tn * 4
    return pl.pallas_call(
        functools.partial(_norm_matmul_kernel, tn=tn),
        out_shape=jax.ShapeDtypeStruct((m, n), BF16),
        grid=(m // tm,),
        in_specs=[
            pl.BlockSpec((tm, d), lambda i: (i, 0)),
            _resident((1, d), lambda i: (0, 0)),
            _resident((d, n), lambda i: (0, 0)),
        ],
        out_specs=pl.BlockSpec((tm, n), lambda i: (i, 0)),
        compiler_params=pltpu.CompilerParams(
            dimension_semantics=("arbitrary",),
            vmem_limit_bytes=_vmem_limit(est)),
        name="norm_matmul",
    )(x2d, g.reshape(1, d), w_bf16)


def _mixer_kernel(*refs, tq, seq, nc, row_groups, n_cast):
    (sink_ref, q_ref, kvc_ref, kvp_ref, kvn_ref, u_ref, cb_ref, up_ref, un_ref,
     cw_ref, ga_ref, gc_ref, wa_ref, wc_ref, wo_ref) = refs[:15]
    cast_src = refs[15:15 + n_cast]
    o_ref = refs[15 + n_cast]
    cast_dst = refs[16 + n_cast:16 + 2 * n_cast]
    kb_ref, vb_ref, attn_ref, conv_ref, m_ref = refs[16 + 2 * n_cast:]
    _run_cast_jobs(cast_src, cast_dst)

    i = pl.program_id(1)
    n_tiles = seq // tq
    nblk = tq // WINDOW_BLOCK
    blk = WINDOW_BLOCK
    hd = HEAD_DIM
    sub = F32_SUBLANE_TILE
    has_prev = i > 0
    has_next = i < n_tiles - 1

    u = u_ref[0].astype(F32)
    last = BF16_SUBLANE_TILE - 1
    u_prev = jnp.where(has_prev, up_ref[0, last:last + 1, :].astype(F32), 0.0)
    u_next = jnp.where(has_next, un_ref[0, 0:1, :].astype(F32), 0.0)
    rolled_dn = pltpu.roll(u, 1, axis=0)
    rolled_up = pltpu.roll(u, tq - 1, axis=0)
    row = lax.broadcasted_iota(jnp.int32, (sub, u.shape[1]), 0)
    u_dn = jnp.concatenate([jnp.where(row == 0, u_prev, rolled_dn[0:sub]), rolled_dn[sub:]], axis=0)
    u_up = jnp.concatenate([rolled_up[:tq - sub], jnp.where(row == sub - 1, u_next, rolled_up[tq - sub:])],
                           axis=0)
    conv = u_dn * cw_ref[0:1, :] + u * cw_ref[1:2, :] + u_up * cw_ref[2:3, :]
    conv_ref[...] = (cb_ref[0].astype(F32) * conv).astype(BF16)

    for h in range(N_KV_HEADS):
        ks = slice(h * hd, (h + 1) * hd)
        vs = slice((N_KV_HEADS + h) * hd, (N_KV_HEADS + h + 1) * hd)
        kb_ref[h, 0:blk, :] = kvp_ref[0, :, ks]
        kb_ref[h, blk:blk + tq, :] = kvc_ref[0, :, ks]
        kb_ref[h, blk + tq:, :] = kvn_ref[0, :, ks]
        vb_ref[h, 0:blk, 0:hd] = kvp_ref[0, :, vs]
        vb_ref[h, blk:blk + tq, 0:hd] = kvc_ref[0, :, vs]
        vb_ref[h, blk + tq:, 0:hd] = kvn_ref[0, :, vs]
        vb_ref[h, :, hd:] = jnp.ones((tq + 2 * blk, hd), BF16)

    rows = Q_GROUP * blk
    qp = lax.broadcasted_iota(jnp.int32, (rows, blk), 0) % blk
    kp = lax.broadcasted_iota(jnp.int32, (rows, blk), 1)
    bias_prev = jnp.where(kp >= qp, 0.0, NEG_INF)
    bias_next = jnp.where(kp <= qp, 0.0, NEG_INF)
    bias_prev_edge = jnp.where(has_prev, bias_prev, NEG_INF)
    bias_next_edge = jnp.where(has_next, bias_next, NEG_INF)

    def attend(h, j, sink_b):
        q_stack = jnp.concatenate(
            [q_ref[0, j * blk:(j + 1) * blk, (h * Q_GROUP + g) * hd:(h * Q_GROUP + g + 1) * hd]
             for g in range(Q_GROUP)], axis=0)
        s = lax.dot_general(q_stack, kb_ref[h, j * blk:(j + 3) * blk, :],
                            (((1,), (1,)), ((), ())), preferred_element_type=F32)
        s_prev = s[:, 0:blk] + (bias_prev_edge if j == 0 else bias_prev)
        s_cur = s[:, blk:2 * blk]
        s_next = s[:, 2 * blk:] + (bias_next_edge if j == nblk - 1 else bias_next)
        m = jnp.max(jnp.maximum(jnp.maximum(s_prev, s_cur), s_next), axis=-1, keepdims=True)
        m = jnp.maximum(m, sink_b)
        e = jnp.concatenate(
            [jnp.exp2(s_prev - m), jnp.exp2(s_cur - m), jnp.exp2(s_next - m)], axis=1)
        o_aug = jnp.dot(e.astype(BF16), vb_ref[h, j * blk:(j + 3) * blk, :],
                        preferred_element_type=F32)
        o = o_aug[:, 0:hd] / (o_aug[:, hd:] + jnp.exp2(sink_b - m))
        for g in range(Q_GROUP):
            qh = h * Q_GROUP + g
            attn_ref[j * blk:(j + 1) * blk, qh * hd:(qh + 1) * hd] = (
                o[g * blk:(g + 1) * blk, :].astype(BF16))

    def projection_steps(rs):
        d = wo_ref.shape[1]

        def merge(cs):
            yc = jnp.dot(conv_ref[rs, :], wc_ref[:, cs], preferred_element_type=F32)
            ya = jnp.dot(attn_ref[rs, :], wa_ref[:, cs], preferred_element_type=F32)
            m_ref[rs, cs] = (ga_ref[0, rs, cs].astype(F32) * ya
                             + gc_ref[0, rs, cs].astype(F32) * yc).astype(BF16)

        def out(cs):
            o_ref[0, rs, cs] = jnp.dot(m_ref[rs, :], wo_ref[:, cs],
                                       preferred_element_type=F32).astype(BF16)

        chunks = [slice(c0, c0 + nc) for c0 in range(0, d, nc)]
        return ([functools.partial(merge, cs) for cs in chunks]
                + [functools.partial(out, cs) for cs in chunks])

    sink_bs = [jnp.concatenate(
        [jnp.full((blk, hd), sink_ref[h * Q_GROUP + g] * LOG2_E, F32) for g in range(Q_GROUP)], axis=0)
        for h in range(N_KV_HEADS)]
    blocks_per_group = nblk // row_groups

    def attention_steps(r):
        return [functools.partial(attend, h, j, sink_bs[h])
                for j in range(r * blocks_per_group, (r + 1) * blocks_per_group)
                for h in range(N_KV_HEADS)]

    for step in attention_steps(0):
        step()
    for r in range(row_groups):
        proj = projection_steps(slice(r * blocks_per_group * blk, (r + 1) * blocks_per_group * blk))
        att = attention_steps(r + 1) if r + 1 < row_groups else []
        for k in range(max(len(proj), len(att))):
            if k < len(proj):
                proj[k]()
            if k < len(att):
                att[k]()


def _mixer(z3, sink, conv_w, wa, wc, wo, cast_f32, *, tq, nc):
    b, s, _ = z3.shape
    d = wo.shape[1]
    aw = ATTN_WIDTH
    kvw = 2 * KV_WIDTH
    q_col = 2 * d // aw
    kv_col = (2 * d + 3 * aw) // kvw
    nblk = tq // WINDOW_BLOCK
    n_win = s // WINDOW_BLOCK
    sub = BF16_SUBLANE_TILE
    n_sub = s // sub

    def tile(col):
        return pl.BlockSpec((1, tq, aw), lambda bi, i, col=col: (bi, i, col))

    in_specs = [
        pl.BlockSpec(memory_space=pltpu.SMEM),
        tile(q_col),
        pl.BlockSpec((1, tq, kvw), lambda bi, i: (bi, i, kv_col)),
        pl.BlockSpec((1, WINDOW_BLOCK, kvw),
                     lambda bi, i: (bi, jnp.maximum(i * nblk - 1, 0), kv_col)),
        pl.BlockSpec((1, WINDOW_BLOCK, kvw),
                     lambda bi, i: (bi, jnp.minimum((i + 1) * nblk, n_win - 1), kv_col)),
        tile(q_col + 1), tile(q_col + 2),
        pl.BlockSpec((1, sub, aw),
                     lambda bi, i: (bi, jnp.maximum(i * (tq // sub) - 1, 0), q_col + 1)),
        pl.BlockSpec((1, sub, aw),
                     lambda bi, i: (bi, jnp.minimum((i + 1) * (tq // sub), n_sub - 1), q_col + 1)),
        _resident((3, aw), lambda bi, i: (0, 0)),
        pl.BlockSpec((1, tq, d), lambda bi, i: (bi, i, 0)),
        pl.BlockSpec((1, tq, d), lambda bi, i: (bi, i, 1)),
        _resident((aw, d), lambda bi, i: (0, 0)),
        _resident((aw, d), lambda bi, i: (0, 0)),
        _resident((d, d), lambda bi, i: (0, 0)),
    ]
    band = tq + 2 * WINDOW_BLOCK
    n_tiles = s // tq
    cast_in, cast_out, cast_shapes = _cast_jobs(cast_f32, b * n_tiles, lambda bi, i: bi * n_tiles + i)
    cast_bytes = sum(2 * spec.block_shape[0] * spec.block_shape[1] * (4 + 2) for spec in cast_in)
    est = (2 * (3 * tq * aw + tq * kvw + 2 * WINDOW_BLOCK * kvw + 2 * sub * aw + 3 * tq * d) * 2
           + (2 * aw * d + d * d) * 2
           + (3 * N_KV_HEADS * band * HEAD_DIM + 2 * tq * aw + tq * d) * 2
           + 4 * tq * aw * 4 + cast_bytes)
    outs = pl.pallas_call(
        functools.partial(_mixer_kernel, tq=tq, seq=s, nc=nc, row_groups=4 if nblk % 4 == 0 else 1,
                          n_cast=len(cast_f32)),
        out_shape=[jax.ShapeDtypeStruct((b, s, d), BF16)] + cast_shapes,
        grid=(b, n_tiles),
        in_specs=in_specs + cast_in,
        out_specs=[pl.BlockSpec((1, tq, d), lambda bi, i: (bi, i, 0))] + cast_out,
        scratch_shapes=[
            pltpu.VMEM((N_KV_HEADS, band, HEAD_DIM), BF16),
            pltpu.VMEM((N_KV_HEADS, band, 2 * HEAD_DIM), BF16),
            pltpu.VMEM((tq, aw), BF16),
            pltpu.VMEM((tq, aw), BF16),
            pltpu.VMEM((tq, d), BF16),
        ],
        compiler_params=pltpu.CompilerParams(
            dimension_semantics=("arbitrary", "arbitrary"),
            vmem_limit_bytes=_vmem_limit(est)),
        name="token_mixer",
    )(sink, z3, z3, z3, z3, z3, z3, z3, z3, conv_w, z3, z3, wa, wc, wo, *cast_f32)
    return outs[0], outs[1:]


def _cross_kernel(x_ref, dl_ref, g_ref, wq_ref, kv_ref, wo_ref, o_ref, a_ref, *, nc):
    hd = MEM_HEAD_DIM
    mw = MEM_HEADS * hd
    n_mem = kv_ref.shape[1]
    d = x_ref.shape[1]
    x1 = x_ref[...] + dl_ref[...].astype(F32)
    h = (x1 * g_ref[...]).astype(BF16)
    inv_rms = lax.rsqrt(jnp.mean(x1 * x1, axis=-1, keepdims=True) + RMS_EPS)
    inv_rms = jnp.concatenate([jnp.broadcast_to(inv_rms, (x1.shape[0], hd))] * MEM_HEADS, axis=1)
    q = jnp.dot(h, wq_ref[...], preferred_element_type=F32) * (inv_rms * (MEM_HEAD_DIM ** -0.5 * LOG2_E))
    q = q.astype(BF16)
    ones = jnp.ones((n_mem, hd), BF16)
    for hh in range(MEM_HEADS):
        k = kv_ref[0, :, hh * hd:(hh + 1) * hd]
        v_aug = jnp.concatenate([kv_ref[0, :, mw + hh * hd:mw + (hh + 1) * hd], ones], axis=1)
        s = lax.dot_general(q[:, hh * hd:(hh + 1) * hd], k, (((1,), (1,)), ((), ())),
                            preferred_element_type=F32)
        e = jnp.exp2(s - jnp.max(s, axis=-1, keepdims=True))
        o_aug = jnp.dot(e.astype(BF16), v_aug, preferred_element_type=F32)
        a_ref[:, hh * hd:(hh + 1) * hd] = (o_aug[:, 0:hd] / o_aug[:, hd:]).astype(BF16)
    for c0 in range(0, d, nc):
        cs = slice(c0, c0 + nc)
        o_ref[:, cs] = (x_ref[:, cs] + dl_ref[:, cs].astype(F32)
                        + jnp.dot(a_ref[...], wo_ref[:, cs], preferred_element_type=F32))


def _cross(x2d, delta2d, g, wq, kv3, wo, *, tm, seq, nc):
    m, d = x2d.shape
    n_mem, kvw = kv3.shape[1], kv3.shape[2]
    mw = wq.shape[1]
    tiles_per_batch = seq // tm
    est = (2 * tm * d * (4 + 2 + 4) + 2 * n_mem * kvw * 2 + 2 * d * mw * 2 + tm * mw * 2
           + tm * d * (4 + 2) + 2 * tm * n_mem * 4 + tm * nc * 4)
    return pl.pallas_call(
        functools.partial(_cross_kernel, nc=nc),
        out_shape=jax.ShapeDtypeStruct((m, d), F32),
        grid=(m // tm,),
        in_specs=[
            pl.BlockSpec((tm, d), lambda i: (i, 0)),
            pl.BlockSpec((tm, d), lambda i: (i, 0)),
            _resident((1, d), lambda i: (0, 0)),
            _resident((d, mw), lambda i: (0, 0)),
            pl.BlockSpec((1, n_mem, kvw), lambda i: (i // tiles_per_batch, 0, 0)),
            _resident((mw, d), lambda i: (0, 0)),
        ],
        out_specs=pl.BlockSpec((tm, d), lambda i: (i, 0)),
        scratch_shapes=[pltpu.VMEM((tm, mw), BF16)],
        compiler_params=pltpu.CompilerParams(
            dimension_semantics=("arbitrary",),
            vmem_limit_bytes=_vmem_limit(est)),
        name="cross_attn",
    )(x2d, delta2d, g.reshape(1, d), wq, kv3, wo)


def _ffn_part_kernel(*refs, chunks, first, final_norm):
    if first:
        y_ref, g_ref, wg_ref, wu_ref, wd_ref, gfin_ref, o_ref, h_ref, a_ref = refs
        h_ref[...] = _rms_norm_f32(y_ref[...], g_ref[...]).astype(BF16)
    else:
        y_ref, h_ref, wg_ref, wu_ref, wd_ref, gfin_ref, o_ref, a_ref = refs
    for c0, cw in chunks:
        cs = slice(c0, c0 + cw)
        gate = jnp.dot(h_ref[...], wg_ref[:, cs], preferred_element_type=F32)
        up = jnp.dot(h_ref[...], wu_ref[:, cs], preferred_element_type=F32)
        a_ref[:, cs] = (jax.nn.silu(gate) * up).astype(BF16)
    y = y_ref[...] + jnp.dot(a_ref[...], wd_ref[...], preferred_element_type=F32)
    if final_norm:
        y = _rms_norm_f32(y, gfin_ref[...])
    o_ref[...] = y


def _ffn_part(y2d, h2d, g, wg, wu, wd, g_final, *, part, n_parts, tm, tc, final_norm):
    m, d = y2d.shape
    fp = wg.shape[1] // n_parts
    first = part == 0
    chunks = tuple((c0, min(tc, fp - c0)) for c0 in range(0, fp, tc))
    row = pl.BlockSpec((tm, d), lambda i: (i, 0))
    vec = _resident((1, d), lambda i: (0, 0))
    weights = [
        _resident((d, fp), lambda i: (0, part)),
        _resident((d, fp), lambda i: (0, part)),
        _resident((fp, d), lambda i: (part, 0)),
    ]
    est = 3 * d * fp * 2 + 2 * tm * d * (4 + 4 + 2) + tm * fp * 2 + 2 * tm * tc * 4 + 2 * tm * d * 4
    if first:
        in_specs = [row, vec] + weights + [vec]
        args = (y2d, g.reshape(1, d), wg, wu, wd, g_final.reshape(1, d))
        out_shape = [jax.ShapeDtypeStruct((m, d), F32), jax.ShapeDtypeStruct((m, d), BF16)]
        out_specs = [row, row]
    else:
        in_specs = [row, row] + weights + [vec]
        args = (y2d, h2d, wg, wu, wd, g_final.reshape(1, d))
        out_shape = jax.ShapeDtypeStruct((m, d), F32)
        out_specs = row
    return pl.pallas_call(
        functools.partial(_ffn_part_kernel, chunks=chunks, first=first, final_norm=final_norm),
        out_shape=out_shape,
        grid=(m // tm,),
        in_specs=in_specs,
        out_specs=out_specs,
        scratch_shapes=[pltpu.VMEM((tm, fp), BF16)],
        compiler_params=pltpu.CompilerParams(
            dimension_semantics=("arbitrary",),
            vmem_limit_bytes=_vmem_limit(est)),
        name="ffn_part%d" % part,
    )(*args)


def _tile_plan(seq):
    return dict(
        in_proj_rows=256,
        mem_rows=256,
        mixer_rows=min(512, seq),
        cross_rows=min(1024, seq),
        ffn_rows=512,
        ffn_parts=2,
        chunk=512,
    )


def _rope_tables(s):
    inv = 1.0 / (ROPE_THETA ** (jnp.arange(0, HEAD_DIM, 2, dtype=F32) / HEAD_DIM))
    ang = jnp.arange(s, dtype=F32)[:, None] * inv[None, :]
    cos, sin = jnp.cos(ang), jnp.sin(ang)
    return jnp.concatenate([cos, cos], axis=-1), jnp.concatenate([-sin, sin], axis=-1)


def kernel(x, mem, g_mix, w_in, sink, conv_w, b_gate, w_attn_out, w_conv_out, w_o,
           g_cross, g_mem, w_cq, w_ckv, w_co, g_ffn, w_gate, w_up, w_down, g_final):
    b, s, d = x.shape
    n_mem = mem.shape[1]
    depth = g_mix.shape[0]
    cos_full, sin_signed = _rope_tables(s)
    x2d = x.reshape(b * s, d)
    mem2d = mem.reshape(b * n_mem, d)
    plan = _tile_plan(s)
    chunk, n_parts = plan["chunk"], plan["ffn_parts"]
    for l in range(depth):
        z, (wa, wc, wo, wq, wco, wckv) = _in_proj(
            x2d, g_mix[l], w_in[l].astype(BF16), b_gate[l], cos_full, sin_signed,
            (w_attn_out[l], w_conv_out[l], w_o[l], w_cq[l], w_co[l], w_ckv[l]),
            tm=plan["in_proj_rows"], tn=chunk, seq=s)
        mem_kv = _norm_matmul(mem2d, g_mem[l], wckv, tm=plan["mem_rows"], tn=chunk)
        delta, (wg, wu, wd) = _mixer(z.reshape(b, s, -1), sink[l], conv_w[l], wa, wc, wo,
                                     (w_gate[l], w_up[l], w_down[l]), tq=plan["mixer_rows"], nc=chunk)
        x2d = _cross(x2d, delta.reshape(b * s, d), g_cross[l], wq, mem_kv.reshape(b, n_mem, -1), wco,
                     tm=plan["cross_rows"], seq=s, nc=chunk)
        h2d = None
        for part in range(n_parts):
            out = _ffn_part(x2d, h2d, g_ffn[l], wg, wu, wd, g_final, part=part, n_parts=n_parts,
                            tm=plan["ffn_rows"], tc=chunk,
                            final_norm=(l == depth - 1 and part == n_parts - 1))
            x2d, h2d = out if part == 0 else (out, h2d)
    return x2d.reshape(b, s, d)
```

```python
import functools
import math

import jax
import jax.numpy as jnp
from jax import lax
from jax.experimental import pallas as pl
from jax.experimental.pallas import tpu as pltpu

HEAD_DIM = 128
N_Q_HEADS = 8
N_KV_HEADS = 2
Q_GROUP = N_Q_HEADS // N_KV_HEADS
ATTN_WIDTH = N_Q_HEADS * HEAD_DIM
KV_WIDTH = N_KV_HEADS * HEAD_DIM
WINDOW_BLOCK = 128
ROPE_THETA = 10000.0
MEM_HEADS = 4
MEM_HEAD_DIM = 128
RMS_EPS = 1e-6
NEG_INF = -1e30
LOG2_E = math.log2(math.e)

V7X_VMEM_BYTES = 64 * 1024 * 1024
VMEM_RESERVE_BYTES = 2 * 1024 * 1024
VMEM_TEMPORARIES_BYTES = 12 * 1024 * 1024
BF16_SUBLANE_TILE = 16
F32_SUBLANE_TILE = 8

F32 = jnp.float32
BF16 = jnp.bfloat16


def _vmem_limit(estimate_bytes):
    return int(min(estimate_bytes + VMEM_TEMPORARIES_BYTES, V7X_VMEM_BYTES - VMEM_RESERVE_BYTES))


def _resident(block_shape, index_map):
    return pl.BlockSpec(block_shape, index_map, pipeline_mode=pl.Buffered(1))


def _rms_norm_f32(x, g):
    y = x * lax.rsqrt(jnp.mean(x * x, axis=-1, keepdims=True) + RMS_EPS)
    return y * g


def _rope(t, cos_full, sin_signed):
    return t * cos_full + pltpu.roll(t, HEAD_DIM // 2, axis=1) * sin_signed


def _cast_jobs(arrays, n_steps, step_index):
    in_specs, out_specs, out_shapes = [], [], []
    for a in arrays:
        r, c = a.shape
        rb = BF16_SUBLANE_TILE
        while r % rb or r // rb > n_steps:
            rb += BF16_SUBLANE_TILE
        n_chunks = r // rb

        def imap(*idx, n_chunks=n_chunks):
            return (jnp.minimum(step_index(*idx), n_chunks - 1), 0)

        in_specs.append(pl.BlockSpec((rb, c), imap))
        out_specs.append(pl.BlockSpec((rb, c), imap))
        out_shapes.append(jax.ShapeDtypeStruct((r, c), BF16))
    return in_specs, out_specs, out_shapes


def _run_cast_jobs(src_refs, dst_refs):
    for src, dst in zip(src_refs, dst_refs):
        dst[...] = src[...].astype(BF16)


def _in_proj_kernel(*refs, tn, d, n_cast):
    x_ref, g_ref, w_ref, bg_ref, cos_ref, sin_ref = refs[:6]
    cast_src = refs[6:6 + n_cast]
    o_ref = refs[6 + n_cast]
    cast_dst = refs[7 + n_cast:7 + 2 * n_cast]
    h_ref = refs[7 + 2 * n_cast]
    _run_cast_jobs(cast_src, cast_dst)

    aw, kw, hd = ATTN_WIDTH, KV_WIDTH, HEAD_DIM
    src_q, src_k, src_cu = 0, aw, aw + 2 * kw
    src_cb, src_cc, src_gl = src_cu + aw, src_cu + 2 * aw, src_cu + 3 * aw
    dst_q, dst_u, dst_cb, dst_kv = 2 * d, 2 * d + aw, 2 * d + 2 * aw, 2 * d + 3 * aw

    x = x_ref[...]
    h_ref[...] = (x * g_ref[...]).astype(BF16)
    inv_rms = lax.rsqrt(jnp.mean(x * x, axis=-1, keepdims=True) + RMS_EPS)
    inv_rms = jnp.broadcast_to(inv_rms, (x.shape[0], HEAD_DIM))
    inv_rms_wide = jnp.concatenate([inv_rms] * (tn // HEAD_DIM), axis=1)

    def proj(col0, width):
        return jnp.dot(h_ref[...], w_ref[:, col0:col0 + width], preferred_element_type=F32)

    cos = cos_ref[...]
    sin = sin_ref[...]
    q_scale = HEAD_DIM ** -0.5 * LOG2_E

    for c0 in range(0, 2 * d, tn):
        o_ref[:, c0:c0 + tn] = jax.nn.sigmoid(
            proj(src_gl + c0, tn) * inv_rms_wide + bg_ref[:, c0:c0 + tn]).astype(BF16)
    acc = proj(src_k, 2 * kw)
    parts = ([_rope(acc[:, k0:k0 + hd], cos, sin) * inv_rms for k0 in range(0, kw, hd)]
             + [acc[:, k0:k0 + hd] * inv_rms for k0 in range(kw, 2 * kw, hd)])
    o_ref[:, dst_kv:] = jnp.concatenate(parts, axis=1).astype(BF16)
    for c0 in range(0, aw, tn):
        acc = proj(src_q + c0, tn)
        heads = [_rope(acc[:, k0:k0 + hd], cos, sin) * (inv_rms * q_scale) for k0 in range(0, tn, hd)]
        o_ref[:, dst_q + c0:dst_q + c0 + tn] = jnp.concatenate(heads, axis=1).astype(BF16)
    for c0 in range(0, aw, tn):
        o_ref[:, dst_u + c0:dst_u + c0 + tn] = (
            (proj(src_cc + c0, tn) * inv_rms_wide) * (proj(src_cu + c0, tn) * inv_rms_wide)).astype(BF16)
    for c0 in range(0, aw, tn):
        o_ref[:, dst_cb + c0:dst_cb + c0 + tn] = (proj(src_cb + c0, tn) * inv_rms_wide).astype(BF16)


def _in_proj(x2d, g, w_bf16, b_gate, cos_full, sin_signed, cast_f32, *, tm, tn, seq):
    m, d = x2d.shape
    n_in = w_bf16.shape[1]
    n_out = n_in - ATTN_WIDTH
    tiles_per_seq = seq // tm
    cast_in, cast_out, cast_shapes = _cast_jobs(cast_f32, m // tm, lambda i: i)
    cast_bytes = sum(2 * spec.block_shape[0] * spec.block_shape[1] * (4 + 2) for spec in cast_in)
    est = (2 * tm * d * 4 + tm * d * 2 + d * n_in * 2 + 2 * tm * n_out * 2 + 4 * tm * HEAD_DIM * 4
           + 4 * tm * tn * 4 + cast_bytes)
    outs = pl.pallas_call(
        functools.partial(_in_proj_kernel, tn=tn, d=d, n_cast=len(cast_f32)),
        out_shape=[jax.ShapeDtypeStruct((m, n_out), BF16)] + cast_shapes,
        grid=(m // tm,),
        in_specs=[
            pl.BlockSpec((tm, d), lambda i: (i, 0)),
            _resident((1, d), lambda i: (0, 0)),
            _resident((d, n_in), lambda i: (0, 0)),
            _resident((1, 2 * d), lambda i: (0, 0)),
            pl.BlockSpec((tm, HEAD_DIM), lambda i: (i % tiles_per_seq, 0)),
            pl.BlockSpec((tm, HEAD_DIM), lambda i: (i % tiles_per_seq, 0)),
        ] + cast_in,
        out_specs=[pl.BlockSpec((tm, n_out), lambda i: (i, 0))] + cast_out,
        scratch_shapes=[pltpu.VMEM((tm, d), BF16)],
        compiler_params=pltpu.CompilerParams(
            dimension_semantics=("arbitrary",),
            vmem_limit_bytes=_vmem_limit(est)),
        name="in_proj",
    )(x2d, g.reshape(1, d), w_bf16, b_gate.reshape(1, 2 * d), cos_full, sin_signed, *cast_f32)
    return outs[0], outs[1:]


def _norm_matmul_kernel(x_ref, g_ref, w_ref, o_ref, *, tn):
    x = x_ref[...]
    h = (x * g_ref[...]).astype(BF16)
    inv_rms = lax.rsqrt(jnp.mean(x * x, axis=-1, keepdims=True) + RMS_EPS)
    inv_rms = jnp.concatenate([jnp.broadcast_to(inv_rms, (x.shape[0], HEAD_DIM))] * (tn // HEAD_DIM), axis=1)
    for c0 in range(0, o_ref.shape[1], tn):
        o_ref[:, c0:c0 + tn] = (jnp.dot(h, w_ref[:, c0:c0 + tn], preferred_element_type=F32)
                                * inv_rms).astype(BF16)


def _norm_matmul(x2d, g, w_bf16, *, tm, tn):
    m, d = x2d.shape
    n = w_bf16.shape[1]
    est = 2 * tm * d * 4 + tm * d * 2 + d * n * 2 + 2 * tm * n * 2 + 2 * tm * tn * 4
    return pl.pallas_call(
        functools.partial(_norm_matmul_kernel, tn=tn),
        out_shape=jax.ShapeDtypeStruct((m, n), BF16),
        grid=(m // tm,),
        in_specs=[
            pl.BlockSpec((tm, d), lambda i: (i, 0)),
            _resident((1, d), lambda i: (0, 0)),
            _resident((d, n), lambda i: (0, 0)),
        ],
        out_specs=pl.BlockSpec((tm, n), lambda i: (i, 0)),
        compiler_params=pltpu.CompilerParams(
            dimension_semantics=("arbitrary",),
            vmem_limit_bytes=_vmem_limit(est)),
        name="norm_matmul",
    )(x2d, g.reshape(1, d), w_bf16)


def _mixer_kernel(sink_ref, q_ref, kvc_ref, kvp_ref, kvn_ref, u_ref, cb_ref, up_ref, un_ref,
                  cw_ref, ga_ref, gc_ref, wa_ref, wc_ref, wo_ref,
                  x_ref, gx_ref, wq_ref, kvm_ref, wco_ref,
                  o_ref,
                  kb_ref, vb_ref, attn_ref, conv_ref, m_ref, ac_ref, *, tq, seq, nc, row_groups):

    i = pl.program_id(1)
    n_tiles = seq // tq
    nblk = tq // WINDOW_BLOCK
    blk = WINDOW_BLOCK
    hd = HEAD_DIM
    sub = F32_SUBLANE_TILE
    has_prev = i > 0
    has_next = i < n_tiles - 1

    u = u_ref[0].astype(F32)
    last = BF16_SUBLANE_TILE - 1
    u_prev = jnp.where(has_prev, up_ref[0, last:last + 1, :].astype(F32), 0.0)
    u_next = jnp.where(has_next, un_ref[0, 0:1, :].astype(F32), 0.0)
    rolled_dn = pltpu.roll(u, 1, axis=0)
    rolled_up = pltpu.roll(u, tq - 1, axis=0)
    row = lax.broadcasted_iota(jnp.int32, (sub, u.shape[1]), 0)
    u_dn = jnp.concatenate([jnp.where(row == 0, u_prev, rolled_dn[0:sub]), rolled_dn[sub:]], axis=0)
    u_up = jnp.concatenate([rolled_up[:tq - sub], jnp.where(row == sub - 1, u_next, rolled_up[tq - sub:])],
                           axis=0)
    conv = u_dn * cw_ref[0:1, :] + u * cw_ref[1:2, :] + u_up * cw_ref[2:3, :]
    conv_ref[...] = (cb_ref[0].astype(F32) * conv).astype(BF16)

    for h in range(N_KV_HEADS):
        ks = slice(h * hd, (h + 1) * hd)
        vs = slice((N_KV_HEADS + h) * hd, (N_KV_HEADS + h + 1) * hd)
        kb_ref[h, 0:blk, :] = kvp_ref[0, :, ks]
        kb_ref[h, blk:blk + tq, :] = kvc_ref[0, :, ks]
        kb_ref[h, blk + tq:, :] = kvn_ref[0, :, ks]
        vb_ref[h, 0:blk, 0:hd] = kvp_ref[0, :, vs]
        vb_ref[h, blk:blk + tq, 0:hd] = kvc_ref[0, :, vs]
        vb_ref[h, blk + tq:, 0:hd] = kvn_ref[0, :, vs]
        vb_ref[h, :, hd:] = jnp.ones((tq + 2 * blk, hd), BF16)

    rows = Q_GROUP * blk
    qp = lax.broadcasted_iota(jnp.int32, (rows, blk), 0) % blk
    kp = lax.broadcasted_iota(jnp.int32, (rows, blk), 1)
    bias_prev = jnp.where(kp >= qp, 0.0, NEG_INF)
    bias_next = jnp.where(kp <= qp, 0.0, NEG_INF)
    bias_prev_edge = jnp.where(has_prev, bias_prev, NEG_INF)
    bias_next_edge = jnp.where(has_next, bias_next, NEG_INF)

    def attend(h, j, sink_b):
        q_stack = jnp.concatenate(
            [q_ref[0, j * blk:(j + 1) * blk, (h * Q_GROUP + g) * hd:(h * Q_GROUP + g + 1) * hd]
             for g in range(Q_GROUP)], axis=0)
        s = lax.dot_general(q_stack, kb_ref[h, j * blk:(j + 3) * blk, :],
                            (((1,), (1,)), ((), ())), preferred_element_type=F32)
        s_prev = s[:, 0:blk] + (bias_prev_edge if j == 0 else bias_prev)
        s_cur = s[:, blk:2 * blk]
        s_next = s[:, 2 * blk:] + (bias_next_edge if j == nblk - 1 else bias_next)
        m = jnp.max(jnp.maximum(jnp.maximum(s_prev, s_cur), s_next), axis=-1, keepdims=True)
        m = jnp.maximum(m, sink_b)
        e = jnp.concatenate(
            [jnp.exp2(s_prev - m), jnp.exp2(s_cur - m), jnp.exp2(s_next - m)], axis=1)
        o_aug = jnp.dot(e.astype(BF16), vb_ref[h, j * blk:(j + 3) * blk, :],
                        preferred_element_type=F32)
        o = o_aug[:, 0:hd] / (o_aug[:, hd:] + jnp.exp2(sink_b - m))
        for g in range(Q_GROUP):
            qh = h * Q_GROUP + g
            attn_ref[j * blk:(j + 1) * blk, qh * hd:(qh + 1) * hd] = (
                o[g * blk:(g + 1) * blk, :].astype(BF16))

    def projection_steps(rs):
        d = wo_ref.shape[1]

        def merge(cs):
            yc = jnp.dot(conv_ref[rs, :], wc_ref[:, cs], preferred_element_type=F32)
            ya = jnp.dot(attn_ref[rs, :], wa_ref[:, cs], preferred_element_type=F32)
            m_ref[rs, cs] = (ga_ref[0, rs, cs].astype(F32) * ya
                             + gc_ref[0, rs, cs].astype(F32) * yc).astype(BF16)

        def out(cs):
            o_ref[0, rs, cs] = x_ref[0, rs, cs] + jnp.dot(m_ref[rs, :], wo_ref[:, cs],
                                                          preferred_element_type=F32)

        chunks = [slice(c0, c0 + nc) for c0 in range(0, d, nc)]
        return ([functools.partial(merge, cs) for cs in chunks]
                + [functools.partial(out, cs) for cs in chunks])

    sink_bs = [jnp.concatenate(
        [jnp.full((blk, hd), sink_ref[h * Q_GROUP + g] * LOG2_E, F32) for g in range(Q_GROUP)], axis=0)
        for h in range(N_KV_HEADS)]
    blocks_per_group = nblk // row_groups

    def attention_steps(r):
        return [functools.partial(attend, h, j, sink_bs[h])
                for j in range(r * blocks_per_group, (r + 1) * blocks_per_group)
                for h in range(N_KV_HEADS)]

    for step in attention_steps(0):
        step()
    for r in range(row_groups):
        proj = projection_steps(slice(r * blocks_per_group * blk, (r + 1) * blocks_per_group * blk))
        att = attention_steps(r + 1) if r + 1 < row_groups else []
        for k in range(max(len(proj), len(att))):
            if k < len(proj):
                proj[k]()
            if k < len(att):
                att[k]()

    mhd = MEM_HEAD_DIM
    mw = MEM_HEADS * mhd
    n_mem = kvm_ref.shape[1]
    x1 = o_ref[0]
    hx = (x1 * gx_ref[...]).astype(BF16)
    inv_rms = lax.rsqrt(jnp.mean(x1 * x1, axis=-1, keepdims=True) + RMS_EPS)
    inv_rms = jnp.concatenate([jnp.broadcast_to(inv_rms, (tq, mhd))] * MEM_HEADS, axis=1)
    qm = jnp.dot(hx, wq_ref[...], preferred_element_type=F32) * (inv_rms * (MEM_HEAD_DIM ** -0.5 * LOG2_E))
    qm = qm.astype(BF16)
    ones = jnp.ones((n_mem, mhd), BF16)
    for hh in range(MEM_HEADS):
        km = kvm_ref[0, :, hh * mhd:(hh + 1) * mhd]
        v_aug = jnp.concatenate([kvm_ref[0, :, mw + hh * mhd:mw + (hh + 1) * mhd], ones], axis=1)
        sm = lax.dot_general(qm[:, hh * mhd:(hh + 1) * mhd], km, (((1,), (1,)), ((), ())),
                             preferred_element_type=F32)
        em = jnp.exp2(sm - jnp.max(sm, axis=-1, keepdims=True))
        o_aug = jnp.dot(em.astype(BF16), v_aug, preferred_element_type=F32)
        ac_ref[:, hh * mhd:(hh + 1) * mhd] = (o_aug[:, 0:mhd] / o_aug[:, mhd:]).astype(BF16)
    d = wo_ref.shape[1]
    for c0 in range(0, d, nc):
        cs = slice(c0, c0 + nc)
        o_ref[0, :, cs] = o_ref[0, :, cs] + jnp.dot(ac_ref[...], wco_ref[:, cs],
                                                     preferred_element_type=F32)


def _mixer(z3, x3, sink, conv_w, wa, wc, wo, g_cross, wq, kv3, wco, *, tq, nc):
    b, s, _ = z3.shape
    d = wo.shape[1]
    aw = ATTN_WIDTH
    kvw = 2 * KV_WIDTH
    q_col = 2 * d // aw
    kv_col = (2 * d + 3 * aw) // kvw
    nblk = tq // WINDOW_BLOCK
    n_win = s // WINDOW_BLOCK
    sub = BF16_SUBLANE_TILE
    n_sub = s // sub
    mw = wq.shape[1]
    n_mem, mkv = kv3.shape[1], kv3.shape[2]

    def tile(col):
        return pl.BlockSpec((1, tq, aw), lambda bi, i, col=col: (bi, i, col))

    in_specs = [
        pl.BlockSpec(memory_space=pltpu.SMEM),
        tile(q_col),
        pl.BlockSpec((1, tq, kvw), lambda bi, i: (bi, i, kv_col)),
        pl.BlockSpec((1, WINDOW_BLOCK, kvw),
                     lambda bi, i: (bi, jnp.maximum(i * nblk - 1, 0), kv_col)),
        pl.BlockSpec((1, WINDOW_BLOCK, kvw),
                     lambda bi, i: (bi, jnp.minimum((i + 1) * nblk, n_win - 1), kv_col)),
        tile(q_col + 1), tile(q_col + 2),
        pl.BlockSpec((1, sub, aw),
                     lambda bi, i: (bi, jnp.maximum(i * (tq // sub) - 1, 0), q_col + 1)),
        pl.BlockSpec((1, sub, aw),
                     lambda bi, i: (bi, jnp.minimum((i + 1) * (tq // sub), n_sub - 1), q_col + 1)),
        _resident((3, aw), lambda bi, i: (0, 0)),
        pl.BlockSpec((1, tq, d), lambda bi, i: (bi, i, 0)),
        pl.BlockSpec((1, tq, d), lambda bi, i: (bi, i, 1)),
        _resident((aw, d), lambda bi, i: (0, 0)),
        _resident((aw, d), lambda bi, i: (0, 0)),
        _resident((d, d), lambda bi, i: (0, 0)),
        pl.BlockSpec((1, tq, d), lambda bi, i: (bi, i, 0)),
        _resident((1, d), lambda bi, i: (0, 0)),
        _resident((d, mw), lambda bi, i: (0, 0)),
        pl.BlockSpec((1, n_mem, mkv), lambda bi, i: (bi, 0, 0)),
        _resident((mw, d), lambda bi, i: (0, 0)),
    ]
    band = tq + 2 * WINDOW_BLOCK
    n_tiles = s // tq
    est = (2 * (3 * tq * aw + tq * kvw + 2 * WINDOW_BLOCK * kvw + 2 * sub * aw + 2 * tq * d) * 2
           + 2 * 2 * tq * d * 4 + (2 * aw * d + d * d + 2 * d * mw) * 2 + 2 * n_mem * mkv * 2
           + (3 * N_KV_HEADS * band * HEAD_DIM + 2 * tq * aw + tq * d + tq * mw) * 2
           + 4 * tq * aw * 4)
    return pl.pallas_call(
        functools.partial(_mixer_kernel, tq=tq, seq=s, nc=nc, row_groups=4 if nblk % 4 == 0 else 1),
        out_shape=jax.ShapeDtypeStruct((b, s, d), F32),
        grid=(b, n_tiles),
        in_specs=in_specs,
        out_specs=pl.BlockSpec((1, tq, d), lambda bi, i: (bi, i, 0)),
        scratch_shapes=[
            pltpu.VMEM((N_KV_HEADS, band, HEAD_DIM), BF16),
            pltpu.VMEM((N_KV_HEADS, band, 2 * HEAD_DIM), BF16),
            pltpu.VMEM((tq, aw), BF16),
            pltpu.VMEM((tq, aw), BF16),
            pltpu.VMEM((tq, d), BF16),
            pltpu.VMEM((tq, mw), BF16),
        ],
        compiler_params=pltpu.CompilerParams(
            dimension_semantics=("arbitrary", "arbitrary"),
            vmem_limit_bytes=_vmem_limit(est)),
        name="token_mixer",
    )(sink, z3, z3, z3, z3, z3, z3, z3, z3, conv_w, z3, z3, wa, wc, wo,
      x3, g_cross.reshape(1, d), wq, kv3, wco)


def _ffn_part_kernel(*refs, chunks, first, final_norm):
    if first:
        y_ref, g_ref, wg_ref, wu_ref, wd_ref, gfin_ref, o_ref, h_ref, a_ref = refs
        h_ref[...] = _rms_norm_f32(y_ref[...], g_ref[...]).astype(BF16)
    else:
        y_ref, h_ref, wg_ref, wu_ref, wd_ref, gfin_ref, o_ref, a_ref = refs
    for c0, cw in chunks:
        cs = slice(c0, c0 + cw)
        gate = jnp.dot(h_ref[...], wg_ref[:, cs], preferred_element_type=F32)
        up = jnp.dot(h_ref[...], wu_ref[:, cs], preferred_element_type=F32)
        a_ref[:, cs] = (jax.nn.silu(gate) * up).astype(BF16)
    y = y_ref[...] + jnp.dot(a_ref[...], wd_ref[...], preferred_element_type=F32)
    if final_norm:
        y = _rms_norm_f32(y, gfin_ref[...])
    o_ref[...] = y


def _ffn_part(y2d, h2d, g, wg, wu, wd, g_final, *, part, n_parts, tm, tc, final_norm):
    m, d = y2d.shape
    fp = wg.shape[1] // n_parts
    first = part == 0
    chunks = tuple((c0, min(tc, fp - c0)) for c0 in range(0, fp, tc))
    row = pl.BlockSpec((tm, d), lambda i: (i, 0))
    vec = _resident((1, d), lambda i: (0, 0))
    weights = [
        _resident((d, fp), lambda i: (0, part)),
        _resident((d, fp), lambda i: (0, part)),
        _resident((fp, d), lambda i: (part, 0)),
    ]
    est = 3 * d * fp * 2 + 2 * tm * d * (4 + 4 + 2) + tm * fp * 2 + 2 * tm * tc * 4 + 2 * tm * d * 4
    if first:
        in_specs = [row, vec] + weights + [vec]
        args = (y2d, g.reshape(1, d), wg, wu, wd, g_final.reshape(1, d))
        out_shape = [jax.ShapeDtypeStruct((m, d), F32), jax.ShapeDtypeStruct((m, d), BF16)]
        out_specs = [row, row]
    else:
        in_specs = [row, row] + weights + [vec]
        args = (y2d, h2d, wg, wu, wd, g_final.reshape(1, d))
        out_shape = jax.ShapeDtypeStruct((m, d), F32)
        out_specs = row
    return pl.pallas_call(
        functools.partial(_ffn_part_kernel, chunks=chunks, first=first, final_norm=final_norm),
        out_shape=out_shape,
        grid=(m // tm,),
        in_specs=in_specs,
        out_specs=out_specs,
        scratch_shapes=[pltpu.VMEM((tm, fp), BF16)],
        compiler_params=pltpu.CompilerParams(
            dimension_semantics=("arbitrary",),
            vmem_limit_bytes=_vmem_limit(est)),
        name="ffn_part%d" % part,
    )(*args)


def _tile_plan(seq):
    return dict(
        in_proj_rows=256,
        mem_rows=256,
        mixer_rows=min(512, seq),
        ffn_rows=512,
        ffn_parts=2,
        chunk=512,
    )


def _rope_tables(s):
    inv = 1.0 / (ROPE_THETA ** (jnp.arange(0, HEAD_DIM, 2, dtype=F32) / HEAD_DIM))
    ang = jnp.arange(s, dtype=F32)[:, None] * inv[None, :]
    cos, sin = jnp.cos(ang), jnp.sin(ang)
    return jnp.concatenate([cos, cos], axis=-1), jnp.concatenate([-sin, sin], axis=-1)


def kernel(x, mem, g_mix, w_in, sink, conv_w, b_gate, w_attn_out, w_conv_out, w_o,
           g_cross, g_mem, w_cq, w_ckv, w_co, g_ffn, w_gate, w_up, w_down, g_final):
    b, s, d = x.shape
    n_mem = mem.shape[1]
    depth = g_mix.shape[0]
    cos_full, sin_signed = _rope_tables(s)
    x2d = x.reshape(b * s, d)
    mem2d = mem.reshape(b * n_mem, d)
    plan = _tile_plan(s)
    chunk, n_parts = plan["chunk"], plan["ffn_parts"]
    for l in range(depth):
        z, (wa, wc, wo, wq, wco, wckv, wg, wu, wd) = _in_proj(
            x2d, g_mix[l], w_in[l].astype(BF16), b_gate[l], cos_full, sin_signed,
            (w_attn_out[l], w_conv_out[l], w_o[l], w_cq[l], w_co[l], w_ckv[l],
             w_gate[l], w_up[l], w_down[l]),
            tm=plan["in_proj_rows"], tn=chunk, seq=s)
        mem_kv = _norm_matmul(mem2d, g_mem[l], wckv, tm=plan["mem_rows"], tn=chunk)
        x2d = _mixer(z.reshape(b, s, -1), x2d.reshape(b, s, d), sink[l], conv_w[l], wa, wc, wo,
                     g_cross[l], wq, mem_kv.reshape(b, n_mem, -1), wco,
                     tq=plan["mixer_rows"], nc=chunk).reshape(b * s, d)
        h2d = None
        for part in range(n_parts):
            out = _ffn_part(x2d, h2d, g_ffn[l], wg, wu, wd, g_final, part=part, n_parts=n_parts,
                            tm=plan["ffn_rows"], tc=chunk,
                            final_norm=(l == depth - 1 and part == n_parts - 1))
            x2d, h2d = out if part == 0 else (out, h2d)
    return x2d.reshape(b, s, d)
```

```python
import functools
import math

import jax
import jax.numpy as jnp
from jax import lax
from jax.experimental import pallas as pl
from jax.experimental.pallas import tpu as pltpu

HEAD_DIM = 128
N_Q_HEADS = 8
N_KV_HEADS = 2
Q_GROUP = N_Q_HEADS // N_KV_HEADS
ATTN_WIDTH = N_Q_HEADS * HEAD_DIM
KV_WIDTH = N_KV_HEADS * HEAD_DIM
WINDOW_BLOCK = 128
ROPE_THETA = 10000.0
MEM_HEADS = 4
MEM_HEAD_DIM = 128
RMS_EPS = 1e-6
NEG_INF = -1e30
LOG2_E = math.log2(math.e)

V7X_VMEM_BYTES = 64 * 1024 * 1024
VMEM_RESERVE_BYTES = 2 * 1024 * 1024
VMEM_TEMPORARIES_BYTES = 12 * 1024 * 1024
BF16_SUBLANE_TILE = 16
F32_SUBLANE_TILE = 8

F32 = jnp.float32
BF16 = jnp.bfloat16


def _vmem_limit(estimate_bytes):
    return int(min(estimate_bytes + VMEM_TEMPORARIES_BYTES, V7X_VMEM_BYTES - VMEM_RESERVE_BYTES))


def _resident(block_shape, index_map):
    return pl.BlockSpec(block_shape, index_map, pipeline_mode=pl.Buffered(1))


def _rms_norm_f32(x, g):
    y = x * lax.rsqrt(jnp.mean(x * x, axis=-1, keepdims=True) + RMS_EPS)
    return y * g


def _rope(t, cos_full, sin_signed):
    return t * cos_full + pltpu.roll(t, HEAD_DIM // 2, axis=1) * sin_signed


def _cast_jobs(arrays, n_steps, step_index):
    in_specs, out_specs, out_shapes = [], [], []
    for a in arrays:
        r, c = a.shape
        rb = BF16_SUBLANE_TILE
        while r % rb or r // rb > n_steps:
            rb += BF16_SUBLANE_TILE
        n_chunks = r // rb

        def imap(*idx, n_chunks=n_chunks):
            return (jnp.minimum(step_index(*idx), n_chunks - 1), 0)

        in_specs.append(pl.BlockSpec((rb, c), imap))
        out_specs.append(pl.BlockSpec((rb, c), imap))
        out_shapes.append(jax.ShapeDtypeStruct((r, c), BF16))
    return in_specs, out_specs, out_shapes


def _run_cast_jobs(src_refs, dst_refs):
    for src, dst in zip(src_refs, dst_refs):
        dst[...] = src[...].astype(BF16)


def _in_proj_kernel(*refs, tn, d, n_cast):
    x_ref, g_ref, w_ref, bg_ref, cos_ref, sin_ref = refs[:6]
    cast_src = refs[6:6 + n_cast]
    o_ref = refs[6 + n_cast]
    cast_dst = refs[7 + n_cast:7 + 2 * n_cast]
    h_ref = refs[7 + 2 * n_cast]
    _run_cast_jobs(cast_src, cast_dst)

    aw, kw, hd = ATTN_WIDTH, KV_WIDTH, HEAD_DIM
    src_q, src_k, src_cu = 0, aw, aw + 2 * kw
    src_cb, src_cc, src_gl = src_cu + aw, src_cu + 2 * aw, src_cu + 3 * aw
    dst_q, dst_u, dst_cb, dst_kv = 2 * d, 2 * d + aw, 2 * d + 2 * aw, 2 * d + 3 * aw

    x = x_ref[...]
    h_ref[...] = (x * g_ref[...]).astype(BF16)
    inv_rms = lax.rsqrt(jnp.mean(x * x, axis=-1, keepdims=True) + RMS_EPS)
    inv_rms = jnp.broadcast_to(inv_rms, (x.shape[0], HEAD_DIM))
    inv_rms_wide = jnp.concatenate([inv_rms] * (tn // HEAD_DIM), axis=1)

    def proj(col0, width):
        return jnp.dot(h_ref[...], w_ref[:, col0:col0 + width], preferred_element_type=F32)

    cos = cos_ref[...]
    sin = sin_ref[...]
    q_scale = HEAD_DIM ** -0.5 * LOG2_E

    for c0 in range(0, 2 * d, tn):
        o_ref[:, c0:c0 + tn] = jax.nn.sigmoid(
            proj(src_gl + c0, tn) * inv_rms_wide + bg_ref[:, c0:c0 + tn]).astype(BF16)
    acc = proj(src_k, 2 * kw)
    parts = ([_rope(acc[:, k0:k0 + hd], cos, sin) * inv_rms for k0 in range(0, kw, hd)]
             + [acc[:, k0:k0 + hd] * inv_rms for k0 in range(kw, 2 * kw, hd)])
    o_ref[:, dst_kv:] = jnp.concatenate(parts, axis=1).astype(BF16)
    for c0 in range(0, aw, tn):
        acc = proj(src_q + c0, tn)
        heads = [_rope(acc[:, k0:k0 + hd], cos, sin) * (inv_rms * q_scale) for k0 in range(0, tn, hd)]
        o_ref[:, dst_q + c0:dst_q + c0 + tn] = jnp.concatenate(heads, axis=1).astype(BF16)
    for c0 in range(0, aw, tn):
        o_ref[:, dst_u + c0:dst_u + c0 + tn] = (
            (proj(src_cc + c0, tn) * inv_rms_wide) * (proj(src_cu + c0, tn) * inv_rms_wide)).astype(BF16)
    for c0 in range(0, aw, tn):
        o_ref[:, dst_cb + c0:dst_cb + c0 + tn] = (proj(src_cb + c0, tn) * inv_rms_wide).astype(BF16)


def _in_proj(x2d, g, w_bf16, b_gate, cos_full, sin_signed, cast_f32, *, tm, tn, seq):
    m, d = x2d.shape
    n_in = w_bf16.shape[1]
    n_out = n_in - ATTN_WIDTH
    tiles_per_seq = seq // tm
    cast_in, cast_out, cast_shapes = _cast_jobs(cast_f32, m // tm, lambda i: i)
    cast_bytes = sum(2 * spec.block_shape[0] * spec.block_shape[1] * (4 + 2) for spec in cast_in)
    est = (2 * tm * d * 4 + tm * d * 2 + d * n_in * 2 + 2 * tm * n_out * 2 + 4 * tm * HEAD_DIM * 4
           + 4 * tm * tn * 4 + cast_bytes)
    outs = pl.pallas_call(
        functools.partial(_in_proj_kernel, tn=tn, d=d, n_cast=len(cast_f32)),
        out_shape=[jax.ShapeDtypeStruct((m, n_out), BF16)] + cast_shapes,
        grid=(m // tm,),
        in_specs=[
            pl.BlockSpec((tm, d), lambda i: (i, 0)),
            _resident((1, d), lambda i: (0, 0)),
            _resident((d, n_in), lambda i: (0, 0)),
            _resident((1, 2 * d), lambda i: (0, 0)),
            pl.BlockSpec((tm, HEAD_DIM), lambda i: (i % tiles_per_seq, 0)),
            pl.BlockSpec((tm, HEAD_DIM), lambda i: (i % tiles_per_seq, 0)),
        ] + cast_in,
        out_specs=[pl.BlockSpec((tm, n_out), lambda i: (i, 0))] + cast_out,
        scratch_shapes=[pltpu.VMEM((tm, d), BF16)],
        compiler_params=pltpu.CompilerParams(
            dimension_semantics=("arbitrary",),
            vmem_limit_bytes=_vmem_limit(est)),
        name="in_proj",
    )(x2d, g.reshape(1, d), w_bf16, b_gate.reshape(1, 2 * d), cos_full, sin_signed, *cast_f32)
    return outs[0], outs[1:]


def _norm_matmul_kernel(*refs, tn, n_cast):
    x_ref, g_ref, w_ref = refs[:3]
    cast_src = refs[3:3 + n_cast]
    o_ref = refs[3 + n_cast]
    cast_dst = refs[4 + n_cast:]
    _run_cast_jobs(cast_src, cast_dst)
    x = x_ref[...]
    h = (x * g_ref[...]).astype(BF16)
    inv_rms = lax.rsqrt(jnp.mean(x * x, axis=-1, keepdims=True) + RMS_EPS)
    inv_rms = jnp.concatenate([jnp.broadcast_to(inv_rms, (x.shape[0], HEAD_DIM))] * (tn // HEAD_DIM), axis=1)
    for c0 in range(0, o_ref.shape[1], tn):
        o_ref[:, c0:c0 + tn] = (jnp.dot(h, w_ref[:, c0:c0 + tn], preferred_element_type=F32)
                                * inv_rms).astype(BF16)


def _norm_matmul(x2d, g, w_bf16, cast_f32, *, tm, tn):
    m, d = x2d.shape
    n = w_bf16.shape[1]
    cast_in, cast_out, cast_shapes = _cast_jobs(cast_f32, m // tm, lambda i: i)
    cast_bytes = sum(2 * spec.block_shape[0] * spec.block_shape[1] * (4 + 2) for spec in cast_in)
    est = 2 * tm * d * 4 + tm * d * 2 + d * n * 2 + 2 * tm * n * 2 + 2 * tm * tn * 4 + cast_bytes
    outs = pl.pallas_call(
        functools.partial(_norm_matmul_kernel, tn=tn, n_cast=len(cast_f32)),
        out_shape=[jax.ShapeDtypeStruct((m, n), BF16)] + cast_shapes,
        grid=(m // tm,),
        in_specs=[
            pl.BlockSpec((tm, d), lambda i: (i, 0)),
            _resident((1, d), lambda i: (0, 0)),
            _resident((d, n), lambda i: (0, 0)),
        ] + cast_in,
        out_specs=[pl.BlockSpec((tm, n), lambda i: (i, 0))] + cast_out,
        compiler_params=pltpu.CompilerParams(
            dimension_semantics=("arbitrary",),
            vmem_limit_bytes=_vmem_limit(est)),
        name="norm_matmul",
    )(x2d, g.reshape(1, d), w_bf16, *cast_f32)
    return outs[0], outs[1:]


def _mixer_kernel(sink_ref, q_ref, kvc_ref, kvp_ref, kvn_ref, u_ref, cb_ref, up_ref, un_ref,
                  cw_ref, ga_ref, gc_ref, wa_ref, wc_ref, wo_ref,
                  x_ref, gx_ref, wq_ref, kvm_ref, wco_ref,
                  o_ref,
                  kb_ref, vb_ref, attn_ref, conv_ref, m_ref, ac_ref, *, tq, seq, nc, row_groups):

    i = pl.program_id(1)
    n_tiles = seq // tq
    nblk = tq // WINDOW_BLOCK
    blk = WINDOW_BLOCK
    hd = HEAD_DIM
    sub = F32_SUBLANE_TILE
    has_prev = i > 0
    has_next = i < n_tiles - 1

    u = u_ref[0].astype(F32)
    last = BF16_SUBLANE_TILE - 1
    u_prev = jnp.where(has_prev, up_ref[0, last:last + 1, :].astype(F32), 0.0)
    u_next = jnp.where(has_next, un_ref[0, 0:1, :].astype(F32), 0.0)
    rolled_dn = pltpu.roll(u, 1, axis=0)
    rolled_up = pltpu.roll(u, tq - 1, axis=0)
    row = lax.broadcasted_iota(jnp.int32, (sub, u.shape[1]), 0)
    u_dn = jnp.concatenate([jnp.where(row == 0, u_prev, rolled_dn[0:sub]), rolled_dn[sub:]], axis=0)
    u_up = jnp.concatenate([rolled_up[:tq - sub], jnp.where(row == sub - 1, u_next, rolled_up[tq - sub:])],
                           axis=0)
    conv = u_dn * cw_ref[0:1, :] + u * cw_ref[1:2, :] + u_up * cw_ref[2:3, :]
    conv_ref[...] = (cb_ref[0].astype(F32) * conv).astype(BF16)

    for h in range(N_KV_HEADS):
        ks = slice(h * hd, (h + 1) * hd)
        vs = slice((N_KV_HEADS + h) * hd, (N_KV_HEADS + h + 1) * hd)
        kb_ref[h, 0:blk, :] = kvp_ref[0, :, ks]
        kb_ref[h, blk:blk + tq, :] = kvc_ref[0, :, ks]
        kb_ref[h, blk + tq:, :] = kvn_ref[0, :, ks]
        vb_ref[h, 0:blk, 0:hd] = kvp_ref[0, :, vs]
        vb_ref[h, blk:blk + tq, 0:hd] = kvc_ref[0, :, vs]
        vb_ref[h, blk + tq:, 0:hd] = kvn_ref[0, :, vs]
        vb_ref[h, :, hd:] = jnp.ones((tq + 2 * blk, hd), BF16)

    rows = Q_GROUP * blk
    qp = lax.broadcasted_iota(jnp.int32, (rows, blk), 0) % blk
    kp = lax.broadcasted_iota(jnp.int32, (rows, blk), 1)
    bias_prev = jnp.where(kp >= qp, 0.0, NEG_INF)
    bias_next = jnp.where(kp <= qp, 0.0, NEG_INF)
    bias_prev_edge = jnp.where(has_prev, bias_prev, NEG_INF)
    bias_next_edge = jnp.where(has_next, bias_next, NEG_INF)

    def attend(h, j, sink_b):
        q_stack = jnp.concatenate(
            [q_ref[0, j * blk:(j + 1) * blk, (h * Q_GROUP + g) * hd:(h * Q_GROUP + g + 1) * hd]
             for g in range(Q_GROUP)], axis=0)
        s = lax.dot_general(q_stack, kb_ref[h, j * blk:(j + 3) * blk, :],
                            (((1,), (1,)), ((), ())), preferred_element_type=F32)
        s_prev = s[:, 0:blk] + (bias_prev_edge if j == 0 else bias_prev)
        s_cur = s[:, blk:2 * blk]
        s_next = s[:, 2 * blk:] + (bias_next_edge if j == nblk - 1 else bias_next)
        m = jnp.max(jnp.maximum(jnp.maximum(s_prev, s_cur), s_next), axis=-1, keepdims=True)
        m = jnp.maximum(m, sink_b)
        e = jnp.concatenate(
            [jnp.exp2(s_prev - m), jnp.exp2(s_cur - m), jnp.exp2(s_next - m)], axis=1)
        o_aug = jnp.dot(e.astype(BF16), vb_ref[h, j * blk:(j + 3) * blk, :],
                        preferred_element_type=F32)
        o = o_aug[:, 0:hd] / (o_aug[:, hd:] + jnp.exp2(sink_b - m))
        for g in range(Q_GROUP):
            qh = h * Q_GROUP + g
            attn_ref[j * blk:(j + 1) * blk, qh * hd:(qh + 1) * hd] = (
                o[g * blk:(g + 1) * blk, :].astype(BF16))

    def projection_steps(rs):
        d = wo_ref.shape[1]

        def merge(cs):
            yc = jnp.dot(conv_ref[rs, :], wc_ref[:, cs], preferred_element_type=F32)
            ya = jnp.dot(attn_ref[rs, :], wa_ref[:, cs], preferred_element_type=F32)
            m_ref[rs, cs] = (ga_ref[0, rs, cs].astype(F32) * ya
                             + gc_ref[0, rs, cs].astype(F32) * yc).astype(BF16)

        def out(cs):
            o_ref[0, rs, cs] = x_ref[0, rs, cs] + jnp.dot(m_ref[rs, :], wo_ref[:, cs],
                                                          preferred_element_type=F32)

        chunks = [slice(c0, c0 + nc) for c0 in range(0, d, nc)]
        return ([functools.partial(merge, cs) for cs in chunks]
                + [functools.partial(out, cs) for cs in chunks])

    sink_bs = [jnp.concatenate(
        [jnp.full((blk, hd), sink_ref[h * Q_GROUP + g] * LOG2_E, F32) for g in range(Q_GROUP)], axis=0)
        for h in range(N_KV_HEADS)]
    blocks_per_group = nblk // row_groups

    def attention_steps(r):
        return [functools.partial(attend, h, j, sink_bs[h])
                for j in range(r * blocks_per_group, (r + 1) * blocks_per_group)
                for h in range(N_KV_HEADS)]

    for step in attention_steps(0):
        step()
    for r in range(row_groups):
        proj = projection_steps(slice(r * blocks_per_group * blk, (r + 1) * blocks_per_group * blk))
        att = attention_steps(r + 1) if r + 1 < row_groups else []
        for k in range(max(len(proj), len(att))):
            if k < len(proj):
                proj[k]()
            if k < len(att):
                att[k]()

    mhd = MEM_HEAD_DIM
    mw = MEM_HEADS * mhd
    n_mem = kvm_ref.shape[1]
    x1 = o_ref[0]
    hx = (x1 * gx_ref[...]).astype(BF16)
    inv_rms = lax.rsqrt(jnp.mean(x1 * x1, axis=-1, keepdims=True) + RMS_EPS)
    inv_rms = jnp.concatenate([jnp.broadcast_to(inv_rms, (tq, mhd))] * MEM_HEADS, axis=1)
    qm = jnp.dot(hx, wq_ref[...], preferred_element_type=F32) * (inv_rms * (MEM_HEAD_DIM ** -0.5 * LOG2_E))
    qm = qm.astype(BF16)
    ones = jnp.ones((n_mem, mhd), BF16)
    for hh in range(MEM_HEADS):
        km = kvm_ref[0, :, hh * mhd:(hh + 1) * mhd]
        v_aug = jnp.concatenate([kvm_ref[0, :, mw + hh * mhd:mw + (hh + 1) * mhd], ones], axis=1)
        sm = lax.dot_general(qm[:, hh * mhd:(hh + 1) * mhd], km, (((1,), (1,)), ((), ())),
                             preferred_element_type=F32)
        em = jnp.exp2(sm - jnp.max(sm, axis=-1, keepdims=True))
        o_aug = jnp.dot(em.astype(BF16), v_aug, preferred_element_type=F32)
        ac_ref[:, hh * mhd:(hh + 1) * mhd] = (o_aug[:, 0:mhd] / o_aug[:, mhd:]).astype(BF16)
    d = wo_ref.shape[1]
    for c0 in range(0, d, nc):
        cs = slice(c0, c0 + nc)
        o_ref[0, :, cs] = o_ref[0, :, cs] + jnp.dot(ac_ref[...], wco_ref[:, cs],
                                                     preferred_element_type=F32)


def _mixer(z3, x3, sink, conv_w, wa, wc, wo, g_cross, wq, kv3, wco, *, tq, nc):
    b, s, _ = z3.shape
    d = wo.shape[1]
    aw = ATTN_WIDTH
    kvw = 2 * KV_WIDTH
    q_col = 2 * d // aw
    kv_col = (2 * d + 3 * aw) // kvw
    nblk = tq // WINDOW_BLOCK
    n_win = s // WINDOW_BLOCK
    sub = BF16_SUBLANE_TILE
    n_sub = s // sub
    mw = wq.shape[1]
    n_mem, mkv = kv3.shape[1], kv3.shape[2]

    def tile(col):
        return pl.BlockSpec((1, tq, aw), lambda bi, i, col=col: (bi, i, col))

    in_specs = [
        pl.BlockSpec(memory_space=pltpu.SMEM),
        tile(q_col),
        pl.BlockSpec((1, tq, kvw), lambda bi, i: (bi, i, kv_col)),
        pl.BlockSpec((1, WINDOW_BLOCK, kvw),
                     lambda bi, i: (bi, jnp.maximum(i * nblk - 1, 0), kv_col)),
        pl.BlockSpec((1, WINDOW_BLOCK, kvw),
                     lambda bi, i: (bi, jnp.minimum((i + 1) * nblk, n_win - 1), kv_col)),
        tile(q_col + 1), tile(q_col + 2),
        pl.BlockSpec((1, sub, aw),
                     lambda bi, i: (bi, jnp.maximum(i * (tq // sub) - 1, 0), q_col + 1)),
        pl.BlockSpec((1, sub, aw),
                     lambda bi, i: (bi, jnp.minimum((i + 1) * (tq // sub), n_sub - 1), q_col + 1)),
        _resident((3, aw), lambda bi, i: (0, 0)),
        pl.BlockSpec((1, tq, d), lambda bi, i: (bi, i, 0)),
        pl.BlockSpec((1, tq, d), lambda bi, i: (bi, i, 1)),
        _resident((aw, d), lambda bi, i: (0, 0)),
        _resident((aw, d), lambda bi, i: (0, 0)),
        _resident((d, d), lambda bi, i: (0, 0)),
        pl.BlockSpec((1, tq, d), lambda bi, i: (bi, i, 0)),
        _resident((1, d), lambda bi, i: (0, 0)),
        _resident((d, mw), lambda bi, i: (0, 0)),
        pl.BlockSpec((1, n_mem, mkv), lambda bi, i: (bi, 0, 0)),
        _resident((mw, d), lambda bi, i: (0, 0)),
    ]
    band = tq + 2 * WINDOW_BLOCK
    n_tiles = s // tq
    est = (2 * (3 * tq * aw + tq * kvw + 2 * WINDOW_BLOCK * kvw + 2 * sub * aw + 2 * tq * d) * 2
           + 2 * 2 * tq * d * 4 + (2 * aw * d + d * d + 2 * d * mw) * 2 + 2 * n_mem * mkv * 2
           + (3 * N_KV_HEADS * band * HEAD_DIM + 2 * tq * aw + tq * d + tq * mw) * 2
           + 4 * tq * aw * 4)
    return pl.pallas_call(
        functools.partial(_mixer_kernel, tq=tq, seq=s, nc=nc, row_groups=4 if nblk % 4 == 0 else 1),
        out_shape=jax.ShapeDtypeStruct((b, s, d), F32),
        grid=(b, n_tiles),
        in_specs=in_specs,
        out_specs=pl.BlockSpec((1, tq, d), lambda bi, i: (bi, i, 0)),
        scratch_shapes=[
            pltpu.VMEM((N_KV_HEADS, band, HEAD_DIM), BF16),
            pltpu.VMEM((N_KV_HEADS, band, 2 * HEAD_DIM), BF16),
            pltpu.VMEM((tq, aw), BF16),
            pltpu.VMEM((tq, aw), BF16),
            pltpu.VMEM((tq, d), BF16),
            pltpu.VMEM((tq, mw), BF16),
        ],
        compiler_params=pltpu.CompilerParams(
            dimension_semantics=("arbitrary", "arbitrary"),
            vmem_limit_bytes=_vmem_limit(est)),
        name="token_mixer",
    )(sink, z3, z3, z3, z3, z3, z3, z3, z3, conv_w, z3, z3, wa, wc, wo,
      x3, g_cross.reshape(1, d), wq, kv3, wco)


def _ffn_part_kernel(*refs, chunks, first, final_norm):
    if first:
        y_ref, g_ref, wg_ref, wu_ref, wd_ref, gfin_ref, o_ref, h_ref, a_ref = refs
        h_ref[...] = _rms_norm_f32(y_ref[...], g_ref[...]).astype(BF16)
    else:
        y_ref, h_ref, wg_ref, wu_ref, wd_ref, gfin_ref, o_ref, a_ref = refs
    for c0, cw in chunks:
        cs = slice(c0, c0 + cw)
        gate = jnp.dot(h_ref[...], wg_ref[:, cs], preferred_element_type=F32)
        up = jnp.dot(h_ref[...], wu_ref[:, cs], preferred_element_type=F32)
        a_ref[:, cs] = (jax.nn.silu(gate) * up).astype(BF16)
    y = y_ref[...] + jnp.dot(a_ref[...], wd_ref[...], preferred_element_type=F32)
    if final_norm:
        y = _rms_norm_f32(y, gfin_ref[...])
    o_ref[...] = y


def _ffn_part(y2d, h2d, g, wg, wu, wd, g_final, *, part, n_parts, tm, tc, final_norm):
    m, d = y2d.shape
    fp = wg.shape[1] // n_parts
    first = part == 0
    chunks = tuple((c0, min(tc, fp - c0)) for c0 in range(0, fp, tc))
    row = pl.BlockSpec((tm, d), lambda i: (i, 0))
    vec = _resident((1, d), lambda i: (0, 0))
    weights = [
        _resident((d, fp), lambda i: (0, part)),
        _resident((d, fp), lambda i: (0, part)),
        _resident((fp, d), lambda i: (part, 0)),
    ]
    est = 3 * d * fp * 2 + 2 * tm * d * (4 + 4 + 2) + tm * fp * 2 + 2 * tm * tc * 4 + 2 * tm * d * 4
    if first:
        in_specs = [row, vec] + weights + [vec]
        args = (y2d, g.reshape(1, d), wg, wu, wd, g_final.reshape(1, d))
        out_shape = [jax.ShapeDtypeStruct((m, d), F32), jax.ShapeDtypeStruct((m, d), BF16)]
        out_specs = [row, row]
    else:
        in_specs = [row, row] + weights + [vec]
        args = (y2d, h2d, wg, wu, wd, g_final.reshape(1, d))
        out_shape = jax.ShapeDtypeStruct((m, d), F32)
        out_specs = row
    return pl.pallas_call(
        functools.partial(_ffn_part_kernel, chunks=chunks, first=first, final_norm=final_norm),
        out_shape=out_shape,
        grid=(m // tm,),
        in_specs=in_specs,
        out_specs=out_specs,
        scratch_shapes=[pltpu.VMEM((tm, fp), BF16)],
        compiler_params=pltpu.CompilerParams(
            dimension_semantics=("arbitrary",),
            vmem_limit_bytes=_vmem_limit(est)),
        name="ffn_part%d" % part,
    )(*args)


def _tile_plan(seq):
    return dict(
        in_proj_rows=256,
        mem_rows=128,
        mixer_rows=min(512, seq),
        ffn_rows=512,
        ffn_parts=2,
        chunk=512,
    )


def _rope_tables(s):
    inv = 1.0 / (ROPE_THETA ** (jnp.arange(0, HEAD_DIM, 2, dtype=F32) / HEAD_DIM))
    ang = jnp.arange(s, dtype=F32)[:, None] * inv[None, :]
    cos, sin = jnp.cos(ang), jnp.sin(ang)
    return jnp.concatenate([cos, cos], axis=-1), jnp.concatenate([-sin, sin], axis=-1)


def kernel(x, mem, g_mix, w_in, sink, conv_w, b_gate, w_attn_out, w_conv_out, w_o,
           g_cross, g_mem, w_cq, w_ckv, w_co, g_ffn, w_gate, w_up, w_down, g_final):
    b, s, d = x.shape
    n_mem = mem.shape[1]
    depth = g_mix.shape[0]
    cos_full, sin_signed = _rope_tables(s)
    x2d = x.reshape(b * s, d)
    mem2d = mem.reshape(b * n_mem, d)
    plan = _tile_plan(s)
    chunk, n_parts = plan["chunk"], plan["ffn_parts"]
    for l in range(depth):
        mem_kv, (w_in_bf16,) = _norm_matmul(mem2d, g_mem[l], w_ckv[l].astype(BF16), (w_in[l],),
                                            tm=plan["mem_rows"], tn=chunk)
        z, (wa, wc, wo, wq, wco, wg, wu, wd) = _in_proj(
            x2d, g_mix[l], w_in_bf16, b_gate[l], cos_full, sin_signed,
            (w_attn_out[l], w_conv_out[l], w_o[l], w_cq[l], w_co[l], w_gate[l], w_up[l], w_down[l]),
            tm=plan["in_proj_rows"], tn=chunk, seq=s)
        x2d = _mixer(z.reshape(b, s, -1), x2d.reshape(b, s, d), sink[l], conv_w[l], wa, wc, wo,
                     g_cross[l], wq, mem_kv.reshape(b, n_mem, -1), wco,
                     tq=plan["mixer_rows"], nc=chunk).reshape(b * s, d)
        h2d = None
        for part in range(n_parts):
            out = _ffn_part(x2d, h2d, g_ffn[l], wg, wu, wd, g_final, part=part, n_parts=n_parts,
                            tm=plan["ffn_rows"], tc=chunk,
                            final_norm=(l == depth - 1 and part == n_parts - 1))
            x2d, h2d = out if part == 0 else (out, h2d)
    return x2d.reshape(b, s, d)
```

```python
import functools
import math

import jax
import jax.numpy as jnp
from jax import lax
from jax.experimental import pallas as pl
from jax.experimental.pallas import tpu as pltpu

HEAD_DIM = 128
N_Q_HEADS = 8
N_KV_HEADS = 2
Q_GROUP = N_Q_HEADS // N_KV_HEADS
ATTN_WIDTH = N_Q_HEADS * HEAD_DIM
KV_WIDTH = N_KV_HEADS * HEAD_DIM
WINDOW_BLOCK = 128
ROPE_THETA = 10000.0
MEM_HEADS = 4
MEM_HEAD_DIM = 128
RMS_EPS = 1e-6
NEG_INF = -1e30
LOG2_E = math.log2(math.e)

V7X_VMEM_BYTES = 64 * 1024 * 1024
VMEM_RESERVE_BYTES = 2 * 1024 * 1024
VMEM_TEMPORARIES_BYTES = 12 * 1024 * 1024
BF16_SUBLANE_TILE = 16
F32_SUBLANE_TILE = 8

F32 = jnp.float32
BF16 = jnp.bfloat16


def _vmem_limit(estimate_bytes):
    return int(min(estimate_bytes + VMEM_TEMPORARIES_BYTES, V7X_VMEM_BYTES - VMEM_RESERVE_BYTES))


def _resident(block_shape, index_map):
    return pl.BlockSpec(block_shape, index_map, pipeline_mode=pl.Buffered(1))


def _rms_norm_f32(x, g):
    y = x * lax.rsqrt(jnp.mean(x * x, axis=-1, keepdims=True) + RMS_EPS)
    return y * g


def _rope(t, cos_full, sin_signed):
    return t * cos_full + pltpu.roll(t, HEAD_DIM // 2, axis=1) * sin_signed


def _cast_jobs(arrays, n_steps, step_index):
    in_specs, out_specs, out_shapes = [], [], []
    for a in arrays:
        r, c = a.shape
        rb = BF16_SUBLANE_TILE
        while r % rb or r // rb > n_steps:
            rb += BF16_SUBLANE_TILE
        n_chunks = r // rb

        def imap(*idx, n_chunks=n_chunks):
            return (jnp.minimum(step_index(*idx), n_chunks - 1), 0)

        in_specs.append(pl.BlockSpec((rb, c), imap))
        out_specs.append(pl.BlockSpec((rb, c), imap))
        out_shapes.append(jax.ShapeDtypeStruct((r, c), BF16))
    return in_specs, out_specs, out_shapes


def _run_cast_jobs(src_refs, dst_refs):
    for src, dst in zip(src_refs, dst_refs):
        dst[...] = src[...].astype(BF16)


def _in_proj_kernel(*refs, tn, d, n_cast):
    x_ref, g_ref, w_ref, bg_ref, cos_ref, sin_ref = refs[:6]
    cast_src = refs[6:6 + n_cast]
    o_ref = refs[6 + n_cast]
    cast_dst = refs[7 + n_cast:7 + 2 * n_cast]
    h_ref = refs[7 + 2 * n_cast]
    _run_cast_jobs(cast_src, cast_dst)

    aw, kw, hd = ATTN_WIDTH, KV_WIDTH, HEAD_DIM
    src_q, src_k, src_cu = 0, aw, aw + 2 * kw
    src_cb, src_cc, src_gl = src_cu + aw, src_cu + 2 * aw, src_cu + 3 * aw
    dst_q, dst_u, dst_cb, dst_kv = 2 * d, 2 * d + aw, 2 * d + 2 * aw, 2 * d + 3 * aw

    x = x_ref[...]
    h_ref[...] = (x * g_ref[...]).astype(BF16)
    inv_rms = lax.rsqrt(jnp.mean(x * x, axis=-1, keepdims=True) + RMS_EPS)
    inv_rms = jnp.broadcast_to(inv_rms, (x.shape[0], HEAD_DIM))
    inv_rms_wide = jnp.concatenate([inv_rms] * (tn // HEAD_DIM), axis=1)

    def proj(col0, width):
        return jnp.dot(h_ref[...], w_ref[:, col0:col0 + width], preferred_element_type=F32)

    cos = cos_ref[...]
    sin = sin_ref[...]
    q_scale = HEAD_DIM ** -0.5 * LOG2_E

    for c0 in range(0, 2 * d, tn):
        o_ref[:, c0:c0 + tn] = jax.nn.sigmoid(
            proj(src_gl + c0, tn) * inv_rms_wide + bg_ref[:, c0:c0 + tn]).astype(BF16)
    acc = proj(src_k, 2 * kw)
    parts = ([_rope(acc[:, k0:k0 + hd], cos, sin) * inv_rms for k0 in range(0, kw, hd)]
             + [acc[:, k0:k0 + hd] * inv_rms for k0 in range(kw, 2 * kw, hd)])
    o_ref[:, dst_kv:] = jnp.concatenate(parts, axis=1).astype(BF16)
    for c0 in range(0, aw, tn):
        acc = proj(src_q + c0, tn)
        heads = [_rope(acc[:, k0:k0 + hd], cos, sin) * (inv_rms * q_scale) for k0 in range(0, tn, hd)]
        o_ref[:, dst_q + c0:dst_q + c0 + tn] = jnp.concatenate(heads, axis=1).astype(BF16)
    for c0 in range(0, aw, tn):
        o_ref[:, dst_u + c0:dst_u + c0 + tn] = (
            (proj(src_cc + c0, tn) * inv_rms_wide) * (proj(src_cu + c0, tn) * inv_rms_wide)).astype(BF16)
    for c0 in range(0, aw, tn):
        o_ref[:, dst_cb + c0:dst_cb + c0 + tn] = (proj(src_cb + c0, tn) * inv_rms_wide).astype(BF16)


def _in_proj(x2d, g, w_bf16, b_gate, cos_full, sin_signed, cast_f32, *, tm, tn, seq):
    m, d = x2d.shape
    n_in = w_bf16.shape[1]
    n_out = n_in - ATTN_WIDTH
    tiles_per_seq = seq // tm
    cast_in, cast_out, cast_shapes = _cast_jobs(cast_f32, m // tm, lambda i: i)
    cast_bytes = sum(2 * spec.block_shape[0] * spec.block_shape[1] * (4 + 2) for spec in cast_in)
    est = (2 * tm * d * 4 + tm * d * 2 + d * n_in * 2 + 2 * tm * n_out * 2 + 4 * tm * HEAD_DIM * 4
           + 4 * tm * tn * 4 + cast_bytes)
    outs = pl.pallas_call(
        functools.partial(_in_proj_kernel, tn=tn, d=d, n_cast=len(cast_f32)),
        out_shape=[jax.ShapeDtypeStruct((m, n_out), BF16)] + cast_shapes,
        grid=(m // tm,),
        in_specs=[
            pl.BlockSpec((tm, d), lambda i: (i, 0)),
            _resident((1, d), lambda i: (0, 0)),
            _resident((d, n_in), lambda i: (0, 0)),
            _resident((1, 2 * d), lambda i: (0, 0)),
            pl.BlockSpec((tm, HEAD_DIM), lambda i: (i % tiles_per_seq, 0)),
            pl.BlockSpec((tm, HEAD_DIM), lambda i: (i % tiles_per_seq, 0)),
        ] + cast_in,
        out_specs=[pl.BlockSpec((tm, n_out), lambda i: (i, 0))] + cast_out,
        scratch_shapes=[pltpu.VMEM((tm, d), BF16)],
        compiler_params=pltpu.CompilerParams(
            dimension_semantics=("arbitrary",),
            vmem_limit_bytes=_vmem_limit(est)),
        name="in_proj",
    )(x2d, g.reshape(1, d), w_bf16, b_gate.reshape(1, 2 * d), cos_full, sin_signed, *cast_f32)
    return outs[0], outs[1:]


def _norm_matmul_kernel(*refs, tn, n_cast):
    x_ref, g_ref, w_ref = refs[:3]
    cast_src = refs[3:3 + n_cast]
    o_ref = refs[3 + n_cast]
    cast_dst = refs[4 + n_cast:]
    _run_cast_jobs(cast_src, cast_dst)
    x = x_ref[...]
    h = (x * g_ref[...]).astype(BF16)
    inv_rms = lax.rsqrt(jnp.mean(x * x, axis=-1, keepdims=True) + RMS_EPS)
    inv_rms = jnp.concatenate([jnp.broadcast_to(inv_rms, (x.shape[0], HEAD_DIM))] * (tn // HEAD_DIM), axis=1)
    for c0 in range(0, o_ref.shape[1], tn):
        o_ref[:, c0:c0 + tn] = (jnp.dot(h, w_ref[:, c0:c0 + tn], preferred_element_type=F32)
                                * inv_rms).astype(BF16)


def _norm_matmul(x2d, g, w_bf16, cast_f32, *, tm, tn):
    m, d = x2d.shape
    n = w_bf16.shape[1]
    cast_in, cast_out, cast_shapes = _cast_jobs(cast_f32, m // tm, lambda i: i)
    cast_bytes = sum(2 * spec.block_shape[0] * spec.block_shape[1] * (4 + 2) for spec in cast_in)
    est = 2 * tm * d * 4 + tm * d * 2 + d * n * 2 + 2 * tm * n * 2 + 2 * tm * tn * 4 + cast_bytes
    outs = pl.pallas_call(
        functools.partial(_norm_matmul_kernel, tn=tn, n_cast=len(cast_f32)),
        out_shape=[jax.ShapeDtypeStruct((m, n), BF16)] + cast_shapes,
        grid=(m // tm,),
        in_specs=[
            pl.BlockSpec((tm, d), lambda i: (i, 0)),
            _resident((1, d), lambda i: (0, 0)),
            _resident((d, n), lambda i: (0, 0)),
        ] + cast_in,
        out_specs=[pl.BlockSpec((tm, n), lambda i: (i, 0))] + cast_out,
        compiler_params=pltpu.CompilerParams(
            dimension_semantics=("arbitrary",),
            vmem_limit_bytes=_vmem_limit(est)),
        name="norm_matmul",
    )(x2d, g.reshape(1, d), w_bf16, *cast_f32)
    return outs[0], outs[1:]


def _mixer_kernel(sink_ref, q_ref, kvc_ref, kvp_ref, kvn_ref, u_ref, cb_ref, up_ref, un_ref,
                  cw_ref, ga_ref, gc_ref, wa_ref, wc_ref, wo_ref,
                  x_ref, gx_ref, wq_ref, kvm_ref, wco_ref,
                  o_ref,
                  kb_ref, vb_ref, attn_ref, conv_ref, m_ref, ac_ref, *, tq, seq, nc, row_groups):

    i = pl.program_id(1)
    n_tiles = seq // tq
    nblk = tq // WINDOW_BLOCK
    blk = WINDOW_BLOCK
    hd = HEAD_DIM
    sub = F32_SUBLANE_TILE
    has_prev = i > 0
    has_next = i < n_tiles - 1

    u = u_ref[0].astype(F32)
    last = BF16_SUBLANE_TILE - 1
    u_prev = jnp.where(has_prev, up_ref[0, last:last + 1, :].astype(F32), 0.0)
    u_next = jnp.where(has_next, un_ref[0, 0:1, :].astype(F32), 0.0)
    rolled_dn = pltpu.roll(u, 1, axis=0)
    rolled_up = pltpu.roll(u, tq - 1, axis=0)
    row = lax.broadcasted_iota(jnp.int32, (sub, u.shape[1]), 0)
    u_dn = jnp.concatenate([jnp.where(row == 0, u_prev, rolled_dn[0:sub]), rolled_dn[sub:]], axis=0)
    u_up = jnp.concatenate([rolled_up[:tq - sub], jnp.where(row == sub - 1, u_next, rolled_up[tq - sub:])],
                           axis=0)
    conv = u_dn * cw_ref[0:1, :] + u * cw_ref[1:2, :] + u_up * cw_ref[2:3, :]
    conv_ref[...] = (cb_ref[0].astype(F32) * conv).astype(BF16)

    for h in range(N_KV_HEADS):
        ks = slice(h * hd, (h + 1) * hd)
        vs = slice((N_KV_HEADS + h) * hd, (N_KV_HEADS + h + 1) * hd)
        kb_ref[h, 0:blk, :] = kvp_ref[0, :, ks]
        kb_ref[h, blk:blk + tq, :] = kvc_ref[0, :, ks]
        kb_ref[h, blk + tq:, :] = kvn_ref[0, :, ks]
        vb_ref[h, 0:blk, 0:hd] = kvp_ref[0, :, vs]
        vb_ref[h, blk:blk + tq, 0:hd] = kvc_ref[0, :, vs]
        vb_ref[h, blk + tq:, 0:hd] = kvn_ref[0, :, vs]
        vb_ref[h, :, hd:] = jnp.ones((tq + 2 * blk, hd), BF16)

    rows = Q_GROUP * blk
    qp = lax.broadcasted_iota(jnp.int32, (rows, blk), 0) % blk
    kp = lax.broadcasted_iota(jnp.int32, (rows, blk), 1)
    bias_prev = jnp.where(kp >= qp, 0.0, NEG_INF)
    bias_next = jnp.where(kp <= qp, 0.0, NEG_INF)
    bias_prev_edge = jnp.where(has_prev, bias_prev, NEG_INF)
    bias_next_edge = jnp.where(has_next, bias_next, NEG_INF)

    def attend(h, j, sink_b):
        q_stack = jnp.concatenate(
            [q_ref[0, j * blk:(j + 1) * blk, (h * Q_GROUP + g) * hd:(h * Q_GROUP + g + 1) * hd]
             for g in range(Q_GROUP)], axis=0)
        s = lax.dot_general(q_stack, kb_ref[h, j * blk:(j + 3) * blk, :],
                            (((1,), (1,)), ((), ())), preferred_element_type=F32)
        s_prev = s[:, 0:blk] + (bias_prev_edge if j == 0 else bias_prev)
        s_cur = s[:, blk:2 * blk]
        s_next = s[:, 2 * blk:] + (bias_next_edge if j == nblk - 1 else bias_next)
        m = jnp.max(jnp.maximum(jnp.maximum(s_prev, s_cur), s_next), axis=-1, keepdims=True)
        m = jnp.maximum(m, sink_b)
        e = jnp.concatenate(
            [jnp.exp2(s_prev - m), jnp.exp2(s_cur - m), jnp.exp2(s_next - m)], axis=1)
        o_aug = jnp.dot(e.astype(BF16), vb_ref[h, j * blk:(j + 3) * blk, :],
                        preferred_element_type=F32)
        o = o_aug[:, 0:hd] / (o_aug[:, hd:] + jnp.exp2(sink_b - m))
        for g in range(Q_GROUP):
            qh = h * Q_GROUP + g
            attn_ref[j * blk:(j + 1) * blk, qh * hd:(qh + 1) * hd] = (
                o[g * blk:(g + 1) * blk, :].astype(BF16))

    def projection_steps(rs):
        d = wo_ref.shape[1]

        def merge(cs):
            yc = jnp.dot(conv_ref[rs, :], wc_ref[:, cs], preferred_element_type=F32)
            ya = jnp.dot(attn_ref[rs, :], wa_ref[:, cs], preferred_element_type=F32)
            m_ref[rs, cs] = (ga_ref[0, rs, cs].astype(F32) * ya
                             + gc_ref[0, rs, cs].astype(F32) * yc).astype(BF16)

        def out(cs):
            o_ref[0, rs, cs] = x_ref[0, rs, cs] + jnp.dot(m_ref[rs, :], wo_ref[:, cs],
                                                          preferred_element_type=F32)

        chunks = [slice(c0, c0 + nc) for c0 in range(0, d, nc)]
        return ([functools.partial(merge, cs) for cs in chunks]
                + [functools.partial(out, cs) for cs in chunks])

    sink_bs = [jnp.concatenate(
        [jnp.full((blk, hd), sink_ref[h * Q_GROUP + g] * LOG2_E, F32) for g in range(Q_GROUP)], axis=0)
        for h in range(N_KV_HEADS)]
    blocks_per_group = nblk // row_groups

    def attention_steps(r):
        return [functools.partial(attend, h, j, sink_bs[h])
                for j in range(r * blocks_per_group, (r + 1) * blocks_per_group)
                for h in range(N_KV_HEADS)]

    for step in attention_steps(0):
        step()
    for r in range(row_groups):
        proj = projection_steps(slice(r * blocks_per_group * blk, (r + 1) * blocks_per_group * blk))
        att = attention_steps(r + 1) if r + 1 < row_groups else []
        for k in range(max(len(proj), len(att))):
            if k < len(proj):
                proj[k]()
            if k < len(att):
                att[k]()

    mhd = MEM_HEAD_DIM
    mw = MEM_HEADS * mhd
    n_mem = kvm_ref.shape[1]
    x1 = o_ref[0]
    hx = (x1 * gx_ref[...]).astype(BF16)
    inv_rms = lax.rsqrt(jnp.mean(x1 * x1, axis=-1, keepdims=True) + RMS_EPS)
    inv_rms = jnp.concatenate([jnp.broadcast_to(inv_rms, (tq, mhd))] * MEM_HEADS, axis=1)
    qm = jnp.dot(hx, wq_ref[...], preferred_element_type=F32) * (inv_rms * (MEM_HEAD_DIM ** -0.5 * LOG2_E))
    qm = qm.astype(BF16)
    ones = jnp.ones((n_mem, mhd), BF16)
    for hh in range(MEM_HEADS):
        km = kvm_ref[0, :, hh * mhd:(hh + 1) * mhd]
        v_aug = jnp.concatenate([kvm_ref[0, :, mw + hh * mhd:mw + (hh + 1) * mhd], ones], axis=1)
        sm = lax.dot_general(qm[:, hh * mhd:(hh + 1) * mhd], km, (((1,), (1,)), ((), ())),
                             preferred_element_type=F32)
        em = jnp.exp2(sm - jnp.max(sm, axis=-1, keepdims=True))
        o_aug = jnp.dot(em.astype(BF16), v_aug, preferred_element_type=F32)
        ac_ref[:, hh * mhd:(hh + 1) * mhd] = (o_aug[:, 0:mhd] / o_aug[:, mhd:]).astype(BF16)
    d = wo_ref.shape[1]
    for c0 in range(0, d, nc):
        cs = slice(c0, c0 + nc)
        o_ref[0, :, cs] = o_ref[0, :, cs] + jnp.dot(ac_ref[...], wco_ref[:, cs],
                                                     preferred_element_type=F32)


def _mixer(z3, x3, sink, conv_w, wa, wc, wo, g_cross, wq, kv3, wco, *, tq, nc):
    b, s, _ = z3.shape
    d = wo.shape[1]
    aw = ATTN_WIDTH
    kvw = 2 * KV_WIDTH
    q_col = 2 * d // aw
    kv_col = (2 * d + 3 * aw) // kvw
    nblk = tq // WINDOW_BLOCK
    n_win = s // WINDOW_BLOCK
    sub = BF16_SUBLANE_TILE
    n_sub = s // sub
    mw = wq.shape[1]
    n_mem, mkv = kv3.shape[1], kv3.shape[2]

    def tile(col):
        return pl.BlockSpec((1, tq, aw), lambda bi, i, col=col: (bi, i, col))

    in_specs = [
        pl.BlockSpec(memory_space=pltpu.SMEM),
        tile(q_col),
        pl.BlockSpec((1, tq, kvw), lambda bi, i: (bi, i, kv_col)),
        pl.BlockSpec((1, WINDOW_BLOCK, kvw),
                     lambda bi, i: (bi, jnp.maximum(i * nblk - 1, 0), kv_col)),
        pl.BlockSpec((1, WINDOW_BLOCK, kvw),
                     lambda bi, i: (bi, jnp.minimum((i + 1) * nblk, n_win - 1), kv_col)),
        tile(q_col + 1), tile(q_col + 2),
        pl.BlockSpec((1, sub, aw),
                     lambda bi, i: (bi, jnp.maximum(i * (tq // sub) - 1, 0), q_col + 1)),
        pl.BlockSpec((1, sub, aw),
                     lambda bi, i: (bi, jnp.minimum((i + 1) * (tq // sub), n_sub - 1), q_col + 1)),
        _resident((3, aw), lambda bi, i: (0, 0)),
        pl.BlockSpec((1, tq, d), lambda bi, i: (bi, i, 0)),
        pl.BlockSpec((1, tq, d), lambda bi, i: (bi, i, 1)),
        _resident((aw, d), lambda bi, i: (0, 0)),
        _resident((aw, d), lambda bi, i: (0, 0)),
        _resident((d, d), lambda bi, i: (0, 0)),
        pl.BlockSpec((1, tq, d), lambda bi, i: (bi, i, 0)),
        _resident((1, d), lambda bi, i: (0, 0)),
        _resident((d, mw), lambda bi, i: (0, 0)),
        pl.BlockSpec((1, n_mem, mkv), lambda bi, i: (bi, 0, 0)),
        _resident((mw, d), lambda bi, i: (0, 0)),
    ]
    band = tq + 2 * WINDOW_BLOCK
    n_tiles = s // tq
    est = (2 * (3 * tq * aw + tq * kvw + 2 * WINDOW_BLOCK * kvw + 2 * sub * aw + 2 * tq * d) * 2
           + 2 * 2 * tq * d * 4 + (2 * aw * d + d * d + 2 * d * mw) * 2 + 2 * n_mem * mkv * 2
           + (3 * N_KV_HEADS * band * HEAD_DIM + 2 * tq * aw + tq * d + tq * mw) * 2
           + 4 * tq * aw * 4)
    return pl.pallas_call(
        functools.partial(_mixer_kernel, tq=tq, seq=s, nc=nc, row_groups=2 if nblk % 2 == 0 else 1),
        out_shape=jax.ShapeDtypeStruct((b, s, d), F32),
        grid=(b, n_tiles),
        in_specs=in_specs,
        out_specs=pl.BlockSpec((1, tq, d), lambda bi, i: (bi, i, 0)),
        scratch_shapes=[
            pltpu.VMEM((N_KV_HEADS, band, HEAD_DIM), BF16),
            pltpu.VMEM((N_KV_HEADS, band, 2 * HEAD_DIM), BF16),
            pltpu.VMEM((tq, aw), BF16),
            pltpu.VMEM((tq, aw), BF16),
            pltpu.VMEM((tq, d), BF16),
            pltpu.VMEM((tq, mw), BF16),
        ],
        compiler_params=pltpu.CompilerParams(
            dimension_semantics=("arbitrary", "arbitrary"),
            vmem_limit_bytes=_vmem_limit(est)),
        name="token_mixer",
    )(sink, z3, z3, z3, z3, z3, z3, z3, z3, conv_w, z3, z3, wa, wc, wo,
      x3, g_cross.reshape(1, d), wq, kv3, wco)


def _ffn_part_kernel(*refs, chunks, first, final_norm):
    if first:
        y_ref, g_ref, wg_ref, wu_ref, wd_ref, gfin_ref, o_ref, h_ref, a_ref = refs
        h_ref[...] = _rms_norm_f32(y_ref[...], g_ref[...]).astype(BF16)
    else:
        y_ref, h_ref, wg_ref, wu_ref, wd_ref, gfin_ref, o_ref, a_ref = refs
    for c0, cw in chunks:
        cs = slice(c0, c0 + cw)
        gate = jnp.dot(h_ref[...], wg_ref[:, cs], preferred_element_type=F32)
        up = jnp.dot(h_ref[...], wu_ref[:, cs], preferred_element_type=F32)
        a_ref[:, cs] = (jax.nn.silu(gate) * up).astype(BF16)
    y = y_ref[...] + jnp.dot(a_ref[...], wd_ref[...], preferred_element_type=F32)
    if final_norm:
        y = _rms_norm_f32(y, gfin_ref[...])
    o_ref[...] = y


def _ffn_part(y2d, h2d, g, wg, wu, wd, g_final, *, part, n_parts, tm, tc, final_norm):
    m, d = y2d.shape
    fp = wg.shape[1] // n_parts
    first = part == 0
    chunks = tuple((c0, min(tc, fp - c0)) for c0 in range(0, fp, tc))
    row = pl.BlockSpec((tm, d), lambda i: (i, 0))
    vec = _resident((1, d), lambda i: (0, 0))
    weights = [
        _resident((d, fp), lambda i: (0, part)),
        _resident((d, fp), lambda i: (0, part)),
        _resident((fp, d), lambda i: (part, 0)),
    ]
    est = 3 * d * fp * 2 + 2 * tm * d * (4 + 4 + 2) + tm * fp * 2 + 2 * tm * tc * 4 + 2 * tm * d * 4
    if first:
        in_specs = [row, vec] + weights + [vec]
        args = (y2d, g.reshape(1, d), wg, wu, wd, g_final.reshape(1, d))
        out_shape = [jax.ShapeDtypeStruct((m, d), F32), jax.ShapeDtypeStruct((m, d), BF16)]
        out_specs = [row, row]
    else:
        in_specs = [row, row] + weights + [vec]
        args = (y2d, h2d, wg, wu, wd, g_final.reshape(1, d))
        out_shape = jax.ShapeDtypeStruct((m, d), F32)
        out_specs = row
    return pl.pallas_call(
        functools.partial(_ffn_part_kernel, chunks=chunks, first=first, final_norm=final_norm),
        out_shape=out_shape,
        grid=(m // tm,),
        in_specs=in_specs,
        out_specs=out_specs,
        scratch_shapes=[pltpu.VMEM((tm, fp), BF16)],
        compiler_params=pltpu.CompilerParams(
            dimension_semantics=("arbitrary",),
            vmem_limit_bytes=_vmem_limit(est)),
        name="ffn_part%d" % part,
    )(*args)


def _tile_plan(seq):
    return dict(
        in_proj_rows=256,
        mem_rows=128,
        mixer_rows=min(512, seq),
        ffn_rows=512,
        ffn_parts=2,
        chunk=512,
    )


def _rope_tables(s):
    inv = 1.0 / (ROPE_THETA ** (jnp.arange(0, HEAD_DIM, 2, dtype=F32) / HEAD_DIM))
    ang = jnp.arange(s, dtype=F32)[:, None] * inv[None, :]
    cos, sin = jnp.cos(ang), jnp.sin(ang)
    return jnp.concatenate([cos, cos], axis=-1), jnp.concatenate([-sin, sin], axis=-1)


def kernel(x, mem, g_mix, w_in, sink, conv_w, b_gate, w_attn_out, w_conv_out, w_o,
           g_cross, g_mem, w_cq, w_ckv, w_co, g_ffn, w_gate, w_up, w_down, g_final):
    b, s, d = x.shape
    n_mem = mem.shape[1]
    depth = g_mix.shape[0]
    cos_full, sin_signed = _rope_tables(s)
    x2d = x.reshape(b * s, d)
    mem2d = mem.reshape(b * n_mem, d)
    plan = _tile_plan(s)
    chunk, n_parts = plan["chunk"], plan["ffn_parts"]
    for l in range(depth):
        mem_kv, (w_in_bf16,) = _norm_matmul(mem2d, g_mem[l], w_ckv[l].astype(BF16), (w_in[l],),
                                            tm=plan["mem_rows"], tn=chunk)
        z, (wa, wc, wo, wq, wco, wg, wu, wd) = _in_proj(
            x2d, g_mix[l], w_in_bf16, b_gate[l], cos_full, sin_signed,
            (w_attn_out[l], w_conv_out[l], w_o[l], w_cq[l], w_co[l], w_gate[l], w_up[l], w_down[l]),
            tm=plan["in_proj_rows"], tn=chunk, seq=s)
        x2d = _mixer(z.reshape(b, s, -1), x2d.reshape(b, s, d), sink[l], conv_w[l], wa, wc, wo,
                     g_cross[l], wq, mem_kv.reshape(b, n_mem, -1), wco,
                     tq=plan["mixer_rows"], nc=chunk).reshape(b * s, d)
        h2d = None
        for part in range(n_parts):
            out = _ffn_part(x2d, h2d, g_ffn[l], wg, wu, wd, g_final, part=part, n_parts=n_parts,
                            tm=plan["ffn_rows"], tc=chunk,
                            final_norm=(l == depth - 1 and part == n_parts - 1))
            x2d, h2d = out if part == 0 else (out, h2d)
    return x2d.reshape(b, s, d)
```

```python
import functools
import math

import jax
import jax.numpy as jnp
from jax import lax
from jax.experimental import pallas as pl
from jax.experimental.pallas import tpu as pltpu

HEAD_DIM = 128
N_Q_HEADS = 8
N_KV_HEADS = 2
Q_GROUP = N_Q_HEADS // N_KV_HEADS
ATTN_WIDTH = N_Q_HEADS * HEAD_DIM
KV_WIDTH = N_KV_HEADS * HEAD_DIM
WINDOW_BLOCK = 128
ROPE_THETA = 10000.0
MEM_HEADS = 4
MEM_HEAD_DIM = 128
RMS_EPS = 1e-6
NEG_INF = -1e30
LOG2_E = math.log2(math.e)

V7X_VMEM_BYTES = 64 * 1024 * 1024
VMEM_RESERVE_BYTES = 2 * 1024 * 1024
VMEM_TEMPORARIES_BYTES = 12 * 1024 * 1024
BF16_SUBLANE_TILE = 16
F32_SUBLANE_TILE = 8

F32 = jnp.float32
BF16 = jnp.bfloat16


def _vmem_limit(estimate_bytes):
    return int(min(estimate_bytes + VMEM_TEMPORARIES_BYTES, V7X_VMEM_BYTES - VMEM_RESERVE_BYTES))


def _resident(block_shape, index_map):
    return pl.BlockSpec(block_shape, index_map, pipeline_mode=pl.Buffered(1))


def _rms_norm_f32(x, g):
    y = x * lax.rsqrt(jnp.mean(x * x, axis=-1, keepdims=True) + RMS_EPS)
    return y * g


def _rope(t, cos_full, sin_signed):
    return t * cos_full + pltpu.roll(t, HEAD_DIM // 2, axis=1) * sin_signed


def _cast_jobs(arrays, n_steps, step_index):
    in_specs, out_specs, out_shapes = [], [], []
    for a in arrays:
        r, c = a.shape
        rb = BF16_SUBLANE_TILE
        while r % rb or r // rb > n_steps:
            rb += BF16_SUBLANE_TILE
        n_chunks = r // rb

        def imap(*idx, n_chunks=n_chunks):
            return (jnp.minimum(step_index(*idx), n_chunks - 1), 0)

        in_specs.append(pl.BlockSpec((rb, c), imap))
        out_specs.append(pl.BlockSpec((rb, c), imap))
        out_shapes.append(jax.ShapeDtypeStruct((r, c), BF16))
    return in_specs, out_specs, out_shapes


def _run_cast_jobs(src_refs, dst_refs):
    for src, dst in zip(src_refs, dst_refs):
        dst[...] = src[...].astype(BF16)


def _in_proj_kernel(*refs, tn, d, n_cast):
    x_ref, g_ref, w_ref, bg_ref, cos_ref, sin_ref = refs[:6]
    cast_src = refs[6:6 + n_cast]
    o_ref = refs[6 + n_cast]
    cast_dst = refs[7 + n_cast:7 + 2 * n_cast]
    h_ref = refs[7 + 2 * n_cast]
    _run_cast_jobs(cast_src, cast_dst)

    aw, kw, hd = ATTN_WIDTH, KV_WIDTH, HEAD_DIM
    src_q, src_k, src_cu = 0, aw, aw + 2 * kw
    src_cb, src_cc, src_gl = src_cu + aw, src_cu + 2 * aw, src_cu + 3 * aw
    dst_q, dst_u, dst_cb, dst_kv = 2 * d, 2 * d + aw, 2 * d + 2 * aw, 2 * d + 3 * aw

    x = x_ref[...]
    h_ref[...] = (x * g_ref[...]).astype(BF16)
    inv_rms = lax.rsqrt(jnp.mean(x * x, axis=-1, keepdims=True) + RMS_EPS)
    inv_rms = jnp.broadcast_to(inv_rms, (x.shape[0], HEAD_DIM))
    inv_rms_wide = jnp.concatenate([inv_rms] * (tn // HEAD_DIM), axis=1)

    def proj(col0, width):
        return jnp.dot(h_ref[...], w_ref[:, col0:col0 + width], preferred_element_type=F32)

    cos = cos_ref[...]
    sin = sin_ref[...]
    q_scale = HEAD_DIM ** -0.5 * LOG2_E

    for c0 in range(0, 2 * d, tn):
        o_ref[:, c0:c0 + tn] = jax.nn.sigmoid(
            proj(src_gl + c0, tn) * inv_rms_wide + bg_ref[:, c0:c0 + tn]).astype(BF16)
    acc = proj(src_k, 2 * kw)
    parts = ([_rope(acc[:, k0:k0 + hd], cos, sin) * inv_rms for k0 in range(0, kw, hd)]
             + [acc[:, k0:k0 + hd] * inv_rms for k0 in range(kw, 2 * kw, hd)])
    o_ref[:, dst_kv:] = jnp.concatenate(parts, axis=1).astype(BF16)
    for c0 in range(0, aw, tn):
        acc = proj(src_q + c0, tn)
        heads = [_rope(acc[:, k0:k0 + hd], cos, sin) * (inv_rms * q_scale) for k0 in range(0, tn, hd)]
        o_ref[:, dst_q + c0:dst_q + c0 + tn] = jnp.concatenate(heads, axis=1).astype(BF16)
    for c0 in range(0, aw, tn):
        o_ref[:, dst_u + c0:dst_u + c0 + tn] = (
            (proj(src_cc + c0, tn) * inv_rms_wide) * (proj(src_cu + c0, tn) * inv_rms_wide)).astype(BF16)
    for c0 in range(0, aw, tn):
        o_ref[:, dst_cb + c0:dst_cb + c0 + tn] = (proj(src_cb + c0, tn) * inv_rms_wide).astype(BF16)


def _in_proj(x2d, g, w_bf16, b_gate, cos_full, sin_signed, cast_f32, *, tm, tn, seq):
    m, d = x2d.shape
    n_in = w_bf16.shape[1]
    n_out = n_in - ATTN_WIDTH
    tiles_per_seq = seq // tm
    cast_in, cast_out, cast_shapes = _cast_jobs(cast_f32, m // tm, lambda i: i)
    cast_bytes = sum(2 * spec.block_shape[0] * spec.block_shape[1] * (4 + 2) for spec in cast_in)
    est = (2 * tm * d * 4 + tm * d * 2 + d * n_in * 2 + 2 * tm * n_out * 2 + 4 * tm * HEAD_DIM * 4
           + 4 * tm * tn * 4 + cast_bytes)
    outs = pl.pallas_call(
        functools.partial(_in_proj_kernel, tn=tn, d=d, n_cast=len(cast_f32)),
        out_shape=[jax.ShapeDtypeStruct((m, n_out), BF16)] + cast_shapes,
        grid=(m // tm,),
        in_specs=[
            pl.BlockSpec((tm, d), lambda i: (i, 0)),
            _resident((1, d), lambda i: (0, 0)),
            _resident((d, n_in), lambda i: (0, 0)),
            _resident((1, 2 * d), lambda i: (0, 0)),
            pl.BlockSpec((tm, HEAD_DIM), lambda i: (i % tiles_per_seq, 0)),
            pl.BlockSpec((tm, HEAD_DIM), lambda i: (i % tiles_per_seq, 0)),
        ] + cast_in,
        out_specs=[pl.BlockSpec((tm, n_out), lambda i: (i, 0))] + cast_out,
        scratch_shapes=[pltpu.VMEM((tm, d), BF16)],
        compiler_params=pltpu.CompilerParams(
            dimension_semantics=("arbitrary",),
            vmem_limit_bytes=_vmem_limit(est)),
        name="in_proj",
    )(x2d, g.reshape(1, d), w_bf16, b_gate.reshape(1, 2 * d), cos_full, sin_signed, *cast_f32)
    return outs[0], outs[1:]


def _norm_matmul_kernel(*refs, tn, n_cast):
    x_ref, g_ref, w_ref = refs[:3]
    cast_src = refs[3:3 + n_cast]
    o_ref = refs[3 + n_cast]
    cast_dst = refs[4 + n_cast:]
    _run_cast_jobs(cast_src, cast_dst)
    x = x_ref[...]
    h = (x * g_ref[...]).astype(BF16)
    inv_rms = lax.rsqrt(jnp.mean(x * x, axis=-1, keepdims=True) + RMS_EPS)
    inv_rms = jnp.concatenate([jnp.broadcast_to(inv_rms, (x.shape[0], HEAD_DIM))] * (tn // HEAD_DIM), axis=1)
    for c0 in range(0, o_ref.shape[1], tn):
        o_ref[:, c0:c0 + tn] = (jnp.dot(h, w_ref[:, c0:c0 + tn], preferred_element_type=F32)
                                * inv_rms).astype(BF16)


def _norm_matmul(x2d, g, w_bf16, cast_f32, *, tm, tn):
    m, d = x2d.shape
    n = w_bf16.shape[1]
    cast_in, cast_out, cast_shapes = _cast_jobs(cast_f32, m // tm, lambda i: i)
    cast_bytes = sum(2 * spec.block_shape[0] * spec.block_shape[1] * (4 + 2) for spec in cast_in)
    est = 2 * tm * d * 4 + tm * d * 2 + d * n * 2 + 2 * tm * n * 2 + 2 * tm * tn * 4 + cast_bytes
    outs = pl.pallas_call(
        functools.partial(_norm_matmul_kernel, tn=tn, n_cast=len(cast_f32)),
        out_shape=[jax.ShapeDtypeStruct((m, n), BF16)] + cast_shapes,
        grid=(m // tm,),
        in_specs=[
            pl.BlockSpec((tm, d), lambda i: (i, 0)),
            _resident((1, d), lambda i: (0, 0)),
            _resident((d, n), lambda i: (0, 0)),
        ] + cast_in,
        out_specs=[pl.BlockSpec((tm, n), lambda i: (i, 0))] + cast_out,
        compiler_params=pltpu.CompilerParams(
            dimension_semantics=("arbitrary",),
            vmem_limit_bytes=_vmem_limit(est)),
        name="norm_matmul",
    )(x2d, g.reshape(1, d), w_bf16, *cast_f32)
    return outs[0], outs[1:]


def _mixer_kernel(sink_ref, q_ref, kvc_ref, kvp_ref, kvn_ref, u_ref, cb_ref, up_ref, un_ref,
                  cw_ref, ga_ref, gc_ref, wa_ref, wc_ref, wo_ref,
                  x_ref, gx_ref, wq_ref, kvm_ref, wco_ref,
                  o_ref,
                  kb_ref, vb_ref, attn_ref, conv_ref, m_ref, ac_ref, *, tq, seq, nc, row_groups):

    i = pl.program_id(1)
    n_tiles = seq // tq
    nblk = tq // WINDOW_BLOCK
    blk = WINDOW_BLOCK
    hd = HEAD_DIM
    sub = F32_SUBLANE_TILE
    has_prev = i > 0
    has_next = i < n_tiles - 1

    u = u_ref[0].astype(F32)
    last = BF16_SUBLANE_TILE - 1
    u_prev = jnp.where(has_prev, up_ref[0, last:last + 1, :].astype(F32), 0.0)
    u_next = jnp.where(has_next, un_ref[0, 0:1, :].astype(F32), 0.0)
    rolled_dn = pltpu.roll(u, 1, axis=0)
    rolled_up = pltpu.roll(u, tq - 1, axis=0)
    row = lax.broadcasted_iota(jnp.int32, (sub, u.shape[1]), 0)
    u_dn = jnp.concatenate([jnp.where(row == 0, u_prev, rolled_dn[0:sub]), rolled_dn[sub:]], axis=0)
    u_up = jnp.concatenate([rolled_up[:tq - sub], jnp.where(row == sub - 1, u_next, rolled_up[tq - sub:])],
                           axis=0)
    conv = u_dn * cw_ref[0:1, :] + u * cw_ref[1:2, :] + u_up * cw_ref[2:3, :]
    conv_ref[...] = (cb_ref[0].astype(F32) * conv).astype(BF16)

    for h in range(N_KV_HEADS):
        ks = slice(h * hd, (h + 1) * hd)
        vs = slice((N_KV_HEADS + h) * hd, (N_KV_HEADS + h + 1) * hd)
        kb_ref[h, 0:blk, :] = kvp_ref[0, :, ks]
        kb_ref[h, blk:blk + tq, :] = kvc_ref[0, :, ks]
        kb_ref[h, blk + tq:, :] = kvn_ref[0, :, ks]
        vb_ref[h, :, 0:blk] = kvp_ref[0, :, vs].astype(F32).T.astype(BF16)
        vb_ref[h, :, blk:blk + tq] = kvc_ref[0, :, vs].astype(F32).T.astype(BF16)
        vb_ref[h, :, blk + tq:] = kvn_ref[0, :, vs].astype(F32).T.astype(BF16)

    cols = Q_GROUP * blk
    kp = lax.broadcasted_iota(jnp.int32, (blk, cols), 0)
    qp = lax.broadcasted_iota(jnp.int32, (blk, cols), 1) % blk
    bias_prev = jnp.where(kp >= qp, 0.0, NEG_INF)
    bias_next = jnp.where(kp <= qp, 0.0, NEG_INF)
    bias_prev_edge = jnp.where(has_prev, bias_prev, NEG_INF)
    bias_next_edge = jnp.where(has_next, bias_next, NEG_INF)

    def attend(h, j, sink_row):
        q_stack = jnp.concatenate(
            [q_ref[0, j * blk:(j + 1) * blk, (h * Q_GROUP + g) * hd:(h * Q_GROUP + g + 1) * hd]
             for g in range(Q_GROUP)], axis=0)
        s = lax.dot_general(kb_ref[h, j * blk:(j + 3) * blk, :], q_stack,
                            (((1,), (1,)), ((), ())), preferred_element_type=F32)
        s_prev = s[0:blk, :] + (bias_prev_edge if j == 0 else bias_prev)
        s_cur = s[blk:2 * blk, :]
        s_next = s[2 * blk:, :] + (bias_next_edge if j == nblk - 1 else bias_next)
        m = jnp.max(jnp.maximum(jnp.maximum(s_prev, s_cur), s_next), axis=0, keepdims=True)
        m = jnp.maximum(m, sink_row)
        e_prev, e_cur, e_next = jnp.exp2(s_prev - m), jnp.exp2(s_cur - m), jnp.exp2(s_next - m)
        denom = jnp.sum(e_prev + e_cur + e_next, axis=0, keepdims=True) + jnp.exp2(sink_row - m)
        e = jnp.concatenate([e_prev, e_cur, e_next], axis=0).astype(BF16)
        o_t = jnp.dot(vb_ref[h, :, j * blk:(j + 3) * blk], e, preferred_element_type=F32) / denom
        for g in range(Q_GROUP):
            qh = h * Q_GROUP + g
            attn_ref[j * blk:(j + 1) * blk, qh * hd:(qh + 1) * hd] = (
                o_t[:, g * blk:(g + 1) * blk].T.astype(BF16))

    def projection_steps(rs):
        d = wo_ref.shape[1]

        def merge(cs):
            yc = jnp.dot(conv_ref[rs, :], wc_ref[:, cs], preferred_element_type=F32)
            ya = jnp.dot(attn_ref[rs, :], wa_ref[:, cs], preferred_element_type=F32)
            m_ref[rs, cs] = (ga_ref[0, rs, cs].astype(F32) * ya
                             + gc_ref[0, rs, cs].astype(F32) * yc).astype(BF16)

        def out(cs):
            o_ref[0, rs, cs] = x_ref[0, rs, cs] + jnp.dot(m_ref[rs, :], wo_ref[:, cs],
                                                          preferred_element_type=F32)

        chunks = [slice(c0, c0 + nc) for c0 in range(0, d, nc)]
        return ([functools.partial(merge, cs) for cs in chunks]
                + [functools.partial(out, cs) for cs in chunks])

    sink_bs = [jnp.concatenate(
        [jnp.full((1, blk), sink_ref[h * Q_GROUP + g] * LOG2_E, F32) for g in range(Q_GROUP)], axis=1)
        for h in range(N_KV_HEADS)]
    blocks_per_group = nblk // row_groups

    def attention_steps(r):
        return [functools.partial(attend, h, j, sink_bs[h])
                for j in range(r * blocks_per_group, (r + 1) * blocks_per_group)
                for h in range(N_KV_HEADS)]

    for step in attention_steps(0):
        step()
    for r in range(row_groups):
        proj = projection_steps(slice(r * blocks_per_group * blk, (r + 1) * blocks_per_group * blk))
        att = attention_steps(r + 1) if r + 1 < row_groups else []
        for k in range(max(len(proj), len(att))):
            if k < len(proj):
                proj[k]()
            if k < len(att):
                att[k]()

    mhd = MEM_HEAD_DIM
    mw = MEM_HEADS * mhd
    n_mem = kvm_ref.shape[1]
    x1 = o_ref[0]
    hx = (x1 * gx_ref[...]).astype(BF16)
    inv_rms = lax.rsqrt(jnp.mean(x1 * x1, axis=-1, keepdims=True) + RMS_EPS)
    inv_rms = jnp.concatenate([jnp.broadcast_to(inv_rms, (tq, mhd))] * MEM_HEADS, axis=1)
    qm = jnp.dot(hx, wq_ref[...], preferred_element_type=F32) * (inv_rms * (MEM_HEAD_DIM ** -0.5 * LOG2_E))
    qm = qm.astype(BF16)
    ones = jnp.ones((n_mem, mhd), BF16)
    for hh in range(MEM_HEADS):
        km = kvm_ref[0, :, hh * mhd:(hh + 1) * mhd]
        v_aug = jnp.concatenate([kvm_ref[0, :, mw + hh * mhd:mw + (hh + 1) * mhd], ones], axis=1)
        sm = lax.dot_general(qm[:, hh * mhd:(hh + 1) * mhd], km, (((1,), (1,)), ((), ())),
                             preferred_element_type=F32)
        em = jnp.exp2(sm - jnp.max(sm, axis=-1, keepdims=True))
        o_aug = jnp.dot(em.astype(BF16), v_aug, preferred_element_type=F32)
        ac_ref[:, hh * mhd:(hh + 1) * mhd] = (o_aug[:, 0:mhd] / o_aug[:, mhd:]).astype(BF16)
    d = wo_ref.shape[1]
    for c0 in range(0, d, nc):
        cs = slice(c0, c0 + nc)
        o_ref[0, :, cs] = o_ref[0, :, cs] + jnp.dot(ac_ref[...], wco_ref[:, cs],
                                                     preferred_element_type=F32)


def _mixer(z3, x3, sink, conv_w, wa, wc, wo, g_cross, wq, kv3, wco, *, tq, nc):
    b, s, _ = z3.shape
    d = wo.shape[1]
    aw = ATTN_WIDTH
    kvw = 2 * KV_WIDTH
    q_col = 2 * d // aw
    kv_col = (2 * d + 3 * aw) // kvw
    nblk = tq // WINDOW_BLOCK
    n_win = s // WINDOW_BLOCK
    sub = BF16_SUBLANE_TILE
    n_sub = s // sub
    mw = wq.shape[1]
    n_mem, mkv = kv3.shape[1], kv3.shape[2]

    def tile(col):
        return pl.BlockSpec((1, tq, aw), lambda bi, i, col=col: (bi, i, col))

    in_specs = [
        pl.BlockSpec(memory_space=pltpu.SMEM),
        tile(q_col),
        pl.BlockSpec((1, tq, kvw), lambda bi, i: (bi, i, kv_col)),
        pl.BlockSpec((1, WINDOW_BLOCK, kvw),
                     lambda bi, i: (bi, jnp.maximum(i * nblk - 1, 0), kv_col)),
        pl.BlockSpec((1, WINDOW_BLOCK, kvw),
                     lambda bi, i: (bi, jnp.minimum((i + 1) * nblk, n_win - 1), kv_col)),
        tile(q_col + 1), tile(q_col + 2),
        pl.BlockSpec((1, sub, aw),
                     lambda bi, i: (bi, jnp.maximum(i * (tq // sub) - 1, 0), q_col + 1)),
        pl.BlockSpec((1, sub, aw),
                     lambda bi, i: (bi, jnp.minimum((i + 1) * (tq // sub), n_sub - 1), q_col + 1)),
        _resident((3, aw), lambda bi, i: (0, 0)),
        pl.BlockSpec((1, tq, d), lambda bi, i: (bi, i, 0)),
        pl.BlockSpec((1, tq, d), lambda bi, i: (bi, i, 1)),
        _resident((aw, d), lambda bi, i: (0, 0)),
        _resident((aw, d), lambda bi, i: (0, 0)),
        _resident((d, d), lambda bi, i: (0, 0)),
        pl.BlockSpec((1, tq, d), lambda bi, i: (bi, i, 0)),
        _resident((1, d), lambda bi, i: (0, 0)),
        _resident((d, mw), lambda bi, i: (0, 0)),
        pl.BlockSpec((1, n_mem, mkv), lambda bi, i: (bi, 0, 0)),
        _resident((mw, d), lambda bi, i: (0, 0)),
    ]
    band = tq + 2 * WINDOW_BLOCK
    n_tiles = s // tq
    est = (2 * (3 * tq * aw + tq * kvw + 2 * WINDOW_BLOCK * kvw + 2 * sub * aw + 2 * tq * d) * 2
           + 2 * 2 * tq * d * 4 + (2 * aw * d + d * d + 2 * d * mw) * 2 + 2 * n_mem * mkv * 2
           + (3 * N_KV_HEADS * band * HEAD_DIM + 2 * tq * aw + tq * d + tq * mw) * 2
           + 4 * tq * aw * 4)
    return pl.pallas_call(
        functools.partial(_mixer_kernel, tq=tq, seq=s, nc=nc, row_groups=2 if nblk % 2 == 0 else 1),
        out_shape=jax.ShapeDtypeStruct((b, s, d), F32),
        grid=(b, n_tiles),
        in_specs=in_specs,
        out_specs=pl.BlockSpec((1, tq, d), lambda bi, i: (bi, i, 0)),
        scratch_shapes=[
            pltpu.VMEM((N_KV_HEADS, band, HEAD_DIM), BF16),
            pltpu.VMEM((N_KV_HEADS, HEAD_DIM, band), BF16),
            pltpu.VMEM((tq, aw), BF16),
            pltpu.VMEM((tq, aw), BF16),
            pltpu.VMEM((tq, d), BF16),
            pltpu.VMEM((tq, mw), BF16),
        ],
        compiler_params=pltpu.CompilerParams(
            dimension_semantics=("arbitrary", "arbitrary"),
            vmem_limit_bytes=_vmem_limit(est)),
        name="token_mixer",
    )(sink, z3, z3, z3, z3, z3, z3, z3, z3, conv_w, z3, z3, wa, wc, wo,
      x3, g_cross.reshape(1, d), wq, kv3, wco)


def _ffn_part_kernel(*refs, chunks, first, final_norm):
    if first:
        y_ref, g_ref, wg_ref, wu_ref, wd_ref, gfin_ref, o_ref, h_ref, a_ref = refs
        h_ref[...] = _rms_norm_f32(y_ref[...], g_ref[...]).astype(BF16)
    else:
        y_ref, h_ref, wg_ref, wu_ref, wd_ref, gfin_ref, o_ref, a_ref = refs
    for c0, cw in chunks:
        cs = slice(c0, c0 + cw)
        gate = jnp.dot(h_ref[...], wg_ref[:, cs], preferred_element_type=F32)
        up = jnp.dot(h_ref[...], wu_ref[:, cs], preferred_element_type=F32)
        a_ref[:, cs] = (jax.nn.silu(gate) * up).astype(BF16)
    y = y_ref[...] + jnp.dot(a_ref[...], wd_ref[...], preferred_element_type=F32)
    if final_norm:
        y = _rms_norm_f32(y, gfin_ref[...])
    o_ref[...] = y


def _ffn_part(y2d, h2d, g, wg, wu, wd, g_final, *, part, n_parts, tm, tc, final_norm):
    m, d = y2d.shape
    fp = wg.shape[1] // n_parts
    first = part == 0
    chunks = tuple((c0, min(tc, fp - c0)) for c0 in range(0, fp, tc))
    row = pl.BlockSpec((tm, d), lambda i: (i, 0))
    vec = _resident((1, d), lambda i: (0, 0))
    weights = [
        _resident((d, fp), lambda i: (0, part)),
        _resident((d, fp), lambda i: (0, part)),
        _resident((fp, d), lambda i: (part, 0)),
    ]
    est = 3 * d * fp * 2 + 2 * tm * d * (4 + 4 + 2) + tm * fp * 2 + 2 * tm * tc * 4 + 2 * tm * d * 4
    if first:
        in_specs = [row, vec] + weights + [vec]
        args = (y2d, g.reshape(1, d), wg, wu, wd, g_final.reshape(1, d))
        out_shape = [jax.ShapeDtypeStruct((m, d), F32), jax.ShapeDtypeStruct((m, d), BF16)]
        out_specs = [row, row]
    else:
        in_specs = [row, row] + weights + [vec]
        args = (y2d, h2d, wg, wu, wd, g_final.reshape(1, d))
        out_shape = jax.ShapeDtypeStruct((m, d), F32)
        out_specs = row
    return pl.pallas_call(
        functools.partial(_ffn_part_kernel, chunks=chunks, first=first, final_norm=final_norm),
        out_shape=out_shape,
        grid=(m // tm,),
        in_specs=in_specs,
        out_specs=out_specs,
        scratch_shapes=[pltpu.VMEM((tm, fp), BF16)],
        compiler_params=pltpu.CompilerParams(
            dimension_semantics=("arbitrary",),
            vmem_limit_bytes=_vmem_limit(est)),
        name="ffn_part%d" % part,
    )(*args)


def _tile_plan(seq):
    return dict(
        in_proj_rows=256,
        mem_rows=128,
        mixer_rows=min(512, seq),
        ffn_rows=512,
        ffn_parts=2,
        chunk=512,
    )


def _rope_tables(s):
    inv = 1.0 / (ROPE_THETA ** (jnp.arange(0, HEAD_DIM, 2, dtype=F32) / HEAD_DIM))
    ang = jnp.arange(s, dtype=F32)[:, None] * inv[None, :]
    cos, sin = jnp.cos(ang), jnp.sin(ang)
    return jnp.concatenate([cos, cos], axis=-1), jnp.concatenate([-sin, sin], axis=-1)


def kernel(x, mem, g_mix, w_in, sink, conv_w, b_gate, w_attn_out, w_conv_out, w_o,
           g_cross, g_mem, w_cq, w_ckv, w_co, g_ffn, w_gate, w_up, w_down, g_final):
    b, s, d = x.shape
    n_mem = mem.shape[1]
    depth = g_mix.shape[0]
    cos_full, sin_signed = _rope_tables(s)
    x2d = x.reshape(b * s, d)
    mem2d = mem.reshape(b * n_mem, d)
    plan = _tile_plan(s)
    chunk, n_parts = plan["chunk"], plan["ffn_parts"]
    for l in range(depth):
        mem_kv, (w_in_bf16,) = _norm_matmul(mem2d, g_mem[l], w_ckv[l].astype(BF16), (w_in[l],),
                                            tm=plan["mem_rows"], tn=chunk)
        z, (wa, wc, wo, wq, wco, wg, wu, wd) = _in_proj(
            x2d, g_mix[l], w_in_bf16, b_gate[l], cos_full, sin_signed,
            (w_attn_out[l], w_conv_out[l], w_o[l], w_cq[l], w_co[l], w_gate[l], w_up[l], w_down[l]),
            tm=plan["in_proj_rows"], tn=chunk, seq=s)
        x2d = _mixer(z.reshape(b, s, -1), x2d.reshape(b, s, d), sink[l], conv_w[l], wa, wc, wo,
                     g_cross[l], wq, mem_kv.reshape(b, n_mem, -1), wco,
                     tq=plan["mixer_rows"], nc=chunk).reshape(b * s, d)
        h2d = None
        for part in range(n_parts):
            out = _ffn_part(x2d, h2d, g_ffn[l], wg, wu, wd, g_final, part=part, n_parts=n_parts,
                            tm=plan["ffn_rows"], tc=chunk,
                            final_norm=(l == depth - 1 and part == n_parts - 1))
            x2d, h2d = out if part == 0 else (out, h2d)
    return x2d.reshape(b, s, d)
```
